```python
import jax, jax.numpy as jnp
from jax import lax
import numpy as np

D_MODEL = 2048
BATCH = 8
SEQ = 2048
DEPTH = 2

CONV_CH = 1024
CONV_K = 3
N_HEADS = 8
HEAD_DIM = 128
N_KV_HEADS = 2
GQA_GROUP = N_HEADS // N_KV_HEADS
ATTN_WIDTH = N_HEADS * HEAD_DIM
KV_WIDTH = N_KV_HEADS * HEAD_DIM
IDX_HEADS = 16
IDX_DIM = 64
TOPK_MAX = 256
Q_BLOCK = 128
N_BRANCHES = 2
IN_WIDTHS = (CONV_CH, CONV_CH, CONV_CH,
             ATTN_WIDTH, KV_WIDTH, KV_WIDTH,
             IDX_HEADS * IDX_DIM, IDX_DIM, IDX_HEADS,
             N_BRANCHES * D_MODEL)
IN_COLS = sum(IN_WIDTHS)
D_FF = 5632
N_EXPERTS = 8
TOP_K_EXPERTS = 2
D_FF_EXPERT = 7168
MOE_BLOCK = 256
N_MOE_LAYERS = DEPTH // 2
N_DENSE_LAYERS = DEPTH - N_MOE_LAYERS
RMS_EPS = 1e-6

kernel_name = "hybrid_conv_dsa_gated_moe_block"


def rmsnorm(x, g):
    xf = x.astype(jnp.float32)
    y = xf * lax.rsqrt(jnp.mean(xf * xf, axis=-1, keepdims=True) + RMS_EPS)
    return (y * g.astype(jnp.float32)).astype(x.dtype)


def split_columns(proj):
    offsets = []
    acc = 0
    for w in IN_WIDTHS[:-1]:
        acc += w
        offsets.append(acc)
    return jnp.split(proj, offsets, axis=-1)


def short_conv_mixer(b_gate, c_gate, u, conv_w):
    v = c_gate * u
    rhs = conv_w.astype(v.dtype)[:, None, :]
    y = lax.conv_general_dilated(v, rhs, window_strides=(1,), padding=[(CONV_K - 1, 0)],
                                 dimension_numbers=('NWC', 'WIO', 'NWC'),
                                 feature_group_count=v.shape[-1])
    return b_gate * y


def dsa_attention(q, k, v, q_idx, k_idx, w_idx):
    bsz, seq = q.shape[0], q.shape[1]
    topk = min(TOPK_MAX, seq // 4)
    n_blk = seq // Q_BLOCK
    key_pos = jnp.arange(seq)

    def to_blocks(a):
        return jnp.moveaxis(a.reshape(bsz, n_blk, Q_BLOCK, *a.shape[2:]), 1, 0)

    def one_block(args):
        qb, qib, wb, blk = args
        q_pos = blk * Q_BLOCK + jnp.arange(Q_BLOCK)
        causal = key_pos[None, :] <= q_pos[:, None]
        dots = jnp.einsum('bqhd,bsd->bqhs', qib, k_idx).astype(jnp.float32) * IDX_DIM ** -0.5
        score = jnp.einsum('bqh,bqhs->bqs', wb.astype(jnp.float32) * IDX_HEADS ** -0.5,
                           jax.nn.relu(dots))
        score = jnp.where(causal[None], score, -jnp.inf)
        _, sel = lax.top_k(score, topk)
        k_sel = jax.vmap(lambda kk, ii: kk[ii])(k, sel)
        v_sel = jax.vmap(lambda vv, ii: vv[ii])(v, sel)
        valid = sel <= q_pos[None, :, None]
        logits = jnp.einsum('bqhgd,bqshd->bqhgs', qb, k_sel).astype(jnp.float32) * HEAD_DIM ** -0.5
        logits = jnp.where(valid[:, :, None, None, :], logits, -jnp.inf)
        p = jax.nn.softmax(logits, axis=-1).astype(v.dtype)
        return jnp.einsum('bqhgs,bqshd->bqhgd', p, v_sel)

    out = lax.map(one_block, (to_blocks(q), to_blocks(q_idx), to_blocks(w_idx), jnp.arange(n_blk)))
    return jnp.moveaxis(out, 0, 1).reshape(bsz, seq, ATTN_WIDTH)


def swiglu(h, w1, w3, w2):
    return (jax.nn.silu(h @ w1) * (h @ w3)) @ w2


def moe_swiglu(h, router_w, w1, w3, w2):
    n_tok = h.shape[0]
    n_assign = n_tok * TOP_K_EXPERTS
    logits = h.astype(jnp.float32) @ router_w.astype(jnp.float32)
    top_val, top_idx = lax.top_k(logits, TOP_K_EXPERTS)
    gates = jax.nn.softmax(top_val, axis=-1)
    flat_e = top_idx.reshape(-1)
    flat_tok = jnp.repeat(jnp.arange(n_tok, dtype=jnp.int32), TOP_K_EXPERTS)
    flat_g = gates.reshape(-1)
    order = jnp.argsort(flat_e)
    s_e, s_tok, s_g = flat_e[order], flat_tok[order], flat_g[order]
    counts = jnp.bincount(flat_e, length=N_EXPERTS)
    padded = ((counts + MOE_BLOCK - 1) // MOE_BLOCK) * MOE_BLOCK
    start_sorted = jnp.cumsum(counts) - counts
    ends_padded = jnp.cumsum(padded)
    start_padded = ends_padded - padded
    dest = start_padded[s_e] + (jnp.arange(n_assign) - start_sorted[s_e])
    n_rows = -(-(n_assign + N_EXPERTS * (MOE_BLOCK - 1)) // MOE_BLOCK) * MOE_BLOCK
    n_blocks = n_rows // MOE_BLOCK
    buf_tok = jnp.zeros((n_rows,), jnp.int32).at[dest].set(s_tok)
    buf_w = jnp.zeros((n_rows,), jnp.float32).at[dest].set(s_g)
    block_e = jnp.clip(jnp.searchsorted(ends_padded, jnp.arange(n_blocks) * MOE_BLOCK, side='right'),
                       0, N_EXPERTS - 1)
    xin = h[buf_tok].reshape(n_blocks, MOE_BLOCK, h.shape[-1])

    def block_ffn(args):
        xb, e = args
        return swiglu(xb, w1[e], w3[e], w2[e])

    yb = lax.map(block_ffn, (xin, block_e)).reshape(n_rows, h.shape[-1])
    yb = yb * buf_w.astype(h.dtype)[:, None]
    return jnp.zeros_like(h).at[buf_tok].add(yb)


def setup_inputs(seed: int = 0) -> dict:
    key = jax.random.key(seed)
    ks = jax.random.split(key, 17)

    def nrm(k, shape, fan_in):
        return jax.random.normal(k, shape, jnp.float32) * (fan_in ** -0.5)

    def gain(k, shape):
        return 1.0 + 0.02 * jax.random.normal(k, shape, jnp.float32)

    return {
        "x": jax.random.normal(ks[0], (BATCH, SEQ, D_MODEL), jnp.float32),
        "mix_norm": gain(ks[1], (DEPTH, D_MODEL)),
        "w_in": nrm(ks[2], (DEPTH, D_MODEL, IN_COLS), D_MODEL),
        "conv_w": nrm(ks[3], (DEPTH, CONV_K, CONV_CH), CONV_K),
        "w_conv_out": nrm(ks[4], (DEPTH, CONV_CH, D_MODEL), CONV_CH),
        "w_attn_out": nrm(ks[5], (DEPTH, ATTN_WIDTH, D_MODEL), ATTN_WIDTH),
        "w_out": nrm(ks[6], (DEPTH, D_MODEL, D_MODEL), D_MODEL),
        "ffn_norm": gain(ks[7], (DEPTH, D_MODEL)),
        "ffn_w1": nrm(ks[8], (N_DENSE_LAYERS, D_MODEL, D_FF), D_MODEL),
        "ffn_w3": nrm(ks[9], (N_DENSE_LAYERS, D_MODEL, D_FF), D_MODEL),
        "ffn_w2": nrm(ks[10], (N_DENSE_LAYERS, D_FF, D_MODEL), D_FF),
        "router_w": nrm(ks[11], (N_MOE_LAYERS, D_MODEL, N_EXPERTS), D_MODEL),
        "exp_w1": nrm(ks[12], (N_MOE_LAYERS, N_EXPERTS, D_MODEL, D_FF_EXPERT), D_MODEL),
        "exp_w3": nrm(ks[13], (N_MOE_LAYERS, N_EXPERTS, D_MODEL, D_FF_EXPERT), D_MODEL),
        "exp_w2": nrm(ks[14], (N_MOE_LAYERS, N_EXPERTS, D_FF_EXPERT, D_MODEL), D_FF_EXPERT),
        "final_norm": gain(ks[15], (D_MODEL,)),
    }


def reference(x, mix_norm, w_in, conv_w, w_conv_out, w_attn_out, w_out, ffn_norm,
              ffn_w1, ffn_w3, ffn_w2, router_w, exp_w1, exp_w3, exp_w2, final_norm):
    bsz, seq, d = x.shape
    for layer in range(DEPTH):
        h = rmsnorm(x, mix_norm[layer])
        proj = h @ w_in[layer]
        b_g, c_g, u, q, k, v, q_i, k_i, w_i, gate_logits = split_columns(proj)
        y_conv = short_conv_mixer(b_g, c_g, u, conv_w[layer]) @ w_conv_out[layer]
        y_attn = dsa_attention(q.reshape(bsz, seq, N_KV_HEADS, GQA_GROUP, HEAD_DIM),
                               k.reshape(bsz, seq, N_KV_HEADS, HEAD_DIM),
                               v.reshape(bsz, seq, N_KV_HEADS, HEAD_DIM),
                               q_i.reshape(bsz, seq, IDX_HEADS, IDX_DIM),
                               k_i, w_i) @ w_attn_out[layer]
        g_conv, g_attn = jnp.split(jax.nn.sigmoid(gate_logits), N_BRANCHES, axis=-1)
        x = x + (g_conv * y_conv + g_attn * y_attn) @ w_out[layer]
        h = rmsnorm(x, ffn_norm[layer])
        if layer % 2 == 0:
            j = layer // 2
            f = swiglu(h, ffn_w1[j], ffn_w3[j], ffn_w2[j])
        else:
            j = layer // 2
            f = moe_swiglu(h.reshape(bsz * seq, d), router_w[j], exp_w1[j], exp_w3[j],
                           exp_w2[j]).reshape(bsz, seq, d)
        x = x + f
    return rmsnorm(x, final_norm)
```

```python
import functools

import jax
import jax.numpy as jnp
from jax import lax
from jax.experimental import pallas as pl
from jax.experimental.pallas import tpu as pltpu

BF16 = jnp.bfloat16
F32 = jnp.float32

D_MODEL = 2048
CONV_CH = 1024
N_HEADS = 8
HEAD_DIM = 128
N_KV_HEADS = 2
GQA_GROUP = N_HEADS // N_KV_HEADS
ATTN_WIDTH = N_HEADS * HEAD_DIM
KV_WIDTH = N_KV_HEADS * HEAD_DIM
IDX_HEADS = 16
IDX_DIM = 64
TOPK_MAX = 256
N_EXPERTS = 8
TOP_K_EXPERTS = 2
RMS_EPS = 1e-6

LANES = 128
VMEM_LIMIT = 58 * 1024 * 1024
INT_MIN = -(2 ** 31)
NEG_BIG = -1e30

Q_TILE = 256
K_TILE = 256
MOE_TILE = 512


def _params(sem):
    return pltpu.CompilerParams(dimension_semantics=sem, vmem_limit_bytes=VMEM_LIMIT)


def _dot(a, b):
    return jnp.dot(a, b, preferred_element_type=F32)


def _dot_nt(a, b):
    return lax.dot_general(a, b, (((1,), (1,)), ((), ())), preferred_element_type=F32)


def _rms(x, g):
    ms = jnp.mean(x * x, axis=-1, keepdims=True)
    return x * lax.rsqrt(ms + RMS_EPS) * g


def _rms_kernel(x_ref, g_ref, o_ref):
    o_ref[...] = _rms(x_ref[...], g_ref[...]).astype(o_ref.dtype)


def rmsnorm(x, g, out_dtype, bm=512):
    n, d = x.shape
    return pl.pallas_call(
        _rms_kernel,
        grid=(n // bm,),
        in_specs=[pl.BlockSpec((bm, d), lambda i: (i, 0)),
                  pl.BlockSpec((1, d), lambda i: (0, 0))],
        out_specs=pl.BlockSpec((bm, d), lambda i: (i, 0)),
        out_shape=jax.ShapeDtypeStruct((n, d), out_dtype),
        compiler_params=_params(("parallel",)),
        name="rmsnorm",
    )(x, g.reshape(1, d))


def _rms_router_kernel(x_ref, g_ref, wr_ref, h_ref, r_ref):
    h = _rms(x_ref[...], g_ref[...]).astype(BF16)
    h_ref[...] = h
    logits = _dot(h, wr_ref[...])
    lane = lax.broadcasted_iota(jnp.int32, logits.shape, 1)
    lane_f = lane.astype(F32)
    lg = jnp.where(lane < N_EXPERTS, logits, -jnp.inf)
    m1 = jnp.max(lg, axis=-1, keepdims=True)
    i1 = jnp.min(jnp.where(lg == m1, lane_f, float(LANES)), axis=-1, keepdims=True)
    lg2 = jnp.where(lane_f == i1, -jnp.inf, lg)
    m2 = jnp.max(lg2, axis=-1, keepdims=True)
    i2 = jnp.min(jnp.where(lg2 == m2, lane_f, float(LANES)), axis=-1, keepdims=True)
    e2 = jnp.exp(m2 - m1)
    den = 1.0 + e2
    g1 = 1.0 / den
    g2 = e2 / den
    r_ref[...] = jnp.where(lane == 0, i1, jnp.where(lane == 1, i2,
                           jnp.where(lane == 2, g1, jnp.where(lane == 3, g2, 0.0))))


def rmsnorm_router(x, g, wr, bm=512):
    n, d = x.shape
    return pl.pallas_call(
        _rms_router_kernel,
        grid=(n // bm,),
        in_specs=[pl.BlockSpec((bm, d), lambda i: (i, 0)),
                  pl.BlockSpec((1, d), lambda i: (0, 0)),
                  pl.BlockSpec((d, LANES), lambda i: (0, 0))],
        out_specs=[pl.BlockSpec((bm, d), lambda i: (i, 0)),
                   pl.BlockSpec((bm, LANES), lambda i: (i, 0))],
        out_shape=[jax.ShapeDtypeStruct((n, d), BF16),
                   jax.ShapeDtypeStruct((n, LANES), F32)],
        compiler_params=_params(("parallel",)),
        name="rmsnorm_router",
    )(x, g.reshape(1, d), wr)


def _mm_kernel(x_ref, w_ref, o_ref):
    o_ref[...] = _dot(x_ref[...], w_ref[...]).astype(o_ref.dtype)


def _mm_res_kernel(x_ref, w_ref, r_ref, o_ref):
    o_ref[...] = (r_ref[...] + _dot(x_ref[...], w_ref[...])).astype(o_ref.dtype)


def matmul(x, w, out_dtype, bm, bn, residual=None, name="matmul"):
    m, k = x.shape
    n = w.shape[1]
    in_specs = [pl.BlockSpec((bm, k), lambda i, j: (i, 0)),
                pl.BlockSpec((k, bn), lambda i, j: (0, j))]
    args = [x, w]
    body = _mm_kernel
    if residual is not None:
        in_specs.append(pl.BlockSpec((bm, bn), lambda i, j: (i, j)))
        args.append(residual)
        body = _mm_res_kernel
    return pl.pallas_call(
        body,
        grid=(m // bm, n // bn),
        in_specs=in_specs,
        out_specs=pl.BlockSpec((bm, bn), lambda i, j: (i, j)),
        out_shape=jax.ShapeDtypeStruct((m, n), out_dtype),
        compiler_params=_params(("parallel", "parallel")),
        name=name,
    )(*args)


def _conv_kernel(h_ref, wb_ref, wc_ref, wu_ref, cw_ref, o_ref):
    h = h_ref[...]
    v = _dot(h, wc_ref[...]) * _dot(h, wu_ref[...])
    row = lax.broadcasted_iota(jnp.int32, v.shape, 0)
    v1 = jnp.where(row >= 1, pltpu.roll(v, 1, 0), 0.0)
    v2 = jnp.where(row >= 2, pltpu.roll(v, 2, 0), 0.0)
    cw = cw_ref[...]
    y = cw[2:3, :] * v + cw[1:2, :] * v1 + cw[0:1, :] * v2
    o_ref[...] = (_dot(h, wb_ref[...]) * y).astype(o_ref.dtype)


def conv_branch(h, wb, wc, wu, conv_w, seq, bn=256):
    n, d = h.shape
    c = wb.shape[1]
    wspec = pl.BlockSpec((d, bn), lambda b, j: (0, j))
    return pl.pallas_call(
        _conv_kernel,
        grid=(n // seq, c // bn),
        in_specs=[pl.BlockSpec((seq, d), lambda b, j: (b, 0)), wspec, wspec, wspec,
                  pl.BlockSpec((conv_w.shape[0], bn), lambda b, j: (0, j))],
        out_specs=pl.BlockSpec((seq, bn), lambda b, j: (b, j)),
        out_shape=jax.ShapeDtypeStruct((n, c), BF16),
        compiler_params=_params(("parallel", "parallel")),
        name="conv_branch",
    )(h, wb, wc, wu, conv_w)


def _mix_kernel(yc_ref, at_ref, h_ref, wco_ref, wao_ref, wgc_ref, wga_ref, o_ref):
    h = h_ref[...]
    g_conv = jax.nn.sigmoid(_dot(h, wgc_ref[...]))
    g_attn = jax.nn.sigmoid(_dot(h, wga_ref[...]))
    y_conv = _dot(yc_ref[...], wco_ref[...])
    y_attn = _dot(at_ref[...], wao_ref[...])
    o_ref[...] = (g_conv * y_conv + g_attn * y_attn).astype(o_ref.dtype)


def gated_mix(yc, at, h, wco, wao, wgc, wga, bm=1024, bn=512):
    n, d = h.shape
    c = yc.shape[1]
    a = at.shape[1]
    return pl.pallas_call(
        _mix_kernel,
        grid=(n // bm, d // bn),
        in_specs=[pl.BlockSpec((bm, c), lambda i, j: (i, 0)),
                  pl.BlockSpec((bm, a), lambda i, j: (i, 0)),
                  pl.BlockSpec((bm, d), lambda i, j: (i, 0)),
                  pl.BlockSpec((c, bn), lambda i, j: (0, j)),
                  pl.BlockSpec((a, bn), lambda i, j: (0, j)),
                  pl.BlockSpec((d, bn), lambda i, j: (0, j)),
                  pl.BlockSpec((d, bn), lambda i, j: (0, j))],
        out_specs=pl.BlockSpec((bm, bn), lambda i, j: (i, j)),
        out_shape=jax.ShapeDtypeStruct((n, d), BF16),
        compiler_params=_params(("parallel", "parallel")),
        name="gated_mix",
    )(yc, at, h, wco, wao, wgc, wga)


def _swiglu_up_kernel(x_ref, w1_ref, w3_ref, o_ref):
    x = x_ref[...]
    o_ref[...] = (jax.nn.silu(_dot(x, w1_ref[...])) * _dot(x, w3_ref[...])).astype(o_ref.dtype)


def swiglu_up(x, w1, w3, bm=1024, bn=512):
    n, d = x.shape
    f = w1.shape[1]
    return pl.pallas_call(
        _swiglu_up_kernel,
        grid=(n // bm, f // bn),
        in_specs=[pl.BlockSpec((bm, d), lambda i, j: (i, 0)),
                  pl.BlockSpec((d, bn), lambda i, j: (0, j)),
                  pl.BlockSpec((d, bn), lambda i, j: (0, j))],
        out_specs=pl.BlockSpec((bm, bn), lambda i, j: (i, j)),
        out_shape=jax.ShapeDtypeStruct((n, f), BF16),
        compiler_params=_params(("parallel", "parallel")),
        name="swiglu_up",
    )(x, w1, w3)


def _dsa_kernel(q_ref, qi_ref, k_ref, v_ref, kbd_ref, wi_ref, o_ref,
                skey_sc, m_sc, l_sc, acc_sc, *, topk):
    qb = pl.program_id(1)
    n_ch = qb + 1
    q_pos = qb * Q_TILE + lax.broadcasted_iota(jnp.int32, (Q_TILE, 1), 0)

    w = wi_ref[...] * (1.0 / 32.0)
    qi = qi_ref[...]

    def score_chunk(c, carry):
        ks = pl.multiple_of(c * K_TILE, K_TILE)
        k_even = kbd_ref[pl.ds(ks, K_TILE), 0:LANES]
        k_odd = kbd_ref[pl.ds(ks, K_TILE), LANES:2 * LANES]
        acc = jnp.zeros((Q_TILE, K_TILE), F32)
        for p in range(IDX_HEADS // 2):
            a = qi[:, p * LANES:(p + 1) * LANES]
            acc = acc + w[:, 2 * p:2 * p + 1] * jnp.maximum(_dot_nt(a, k_even), 0.0)
            acc = acc + w[:, 2 * p + 1:2 * p + 2] * jnp.maximum(_dot_nt(a, k_odd), 0.0)
        key_pos = ks + lax.broadcasted_iota(jnp.int32, (1, K_TILE), 1)
        bits = pltpu.bitcast(acc, jnp.int32)
        skey = bits ^ ((bits >> 31) & 0x7FFFFFFF)
        skey_sc[c] = jnp.where(key_pos <= q_pos, skey, INT_MIN)
        return carry

    lax.fori_loop(0, n_ch, score_chunk, 0)

    def count_ge(cand):
        def body(c, cnt):
            ge = jnp.where(skey_sc[c] >= cand, 1.0, 0.0)
            return cnt + jnp.sum(ge, axis=-1, keepdims=True)
        return lax.fori_loop(0, n_ch, body, jnp.zeros((Q_TILE, 1), F32))

    kf = float(topk)
    tau0 = jnp.where(count_ge(jnp.zeros((Q_TILE, 1), jnp.int32)) >= kf, 0, INT_MIN)

    def bit_step(i, tau):
        cand = tau | lax.shift_left(jnp.int32(1), 30 - i)
        return jnp.where(count_ge(cand) >= kf, cand, tau)

    tau = lax.fori_loop(0, 31, bit_step, tau0)
    tau = jnp.maximum(tau, INT_MIN + 1)

    scale = HEAD_DIM ** -0.5
    q = q_ref[...]
    rows = GQA_GROUP * Q_TILE
    for g in range(N_KV_HEADS):
        qg = jnp.concatenate(
            [q[:, (g * GQA_GROUP + j) * HEAD_DIM:(g * GQA_GROUP + j + 1) * HEAD_DIM]
             for j in range(GQA_GROUP)], axis=0)
        m_sc[...] = jnp.full((rows, 1), NEG_BIG, F32)
        l_sc[...] = jnp.zeros((rows, 1), F32)
        acc_sc[...] = jnp.zeros((rows, HEAD_DIM), F32)

        def attn_chunk(c, carry):
            ks = pl.multiple_of(c * K_TILE, K_TILE)
            kc = k_ref[pl.ds(ks, K_TILE), g * HEAD_DIM:(g + 1) * HEAD_DIM]
            vc = v_ref[pl.ds(ks, K_TILE), g * HEAD_DIM:(g + 1) * HEAD_DIM]
            bias = jnp.where(skey_sc[c] >= tau, 0.0, NEG_BIG)
            bias = jnp.concatenate([bias] * GQA_GROUP, axis=0)
            s = _dot_nt(qg, kc) * scale + bias
            m_old = m_sc[...]
            m_new = jnp.maximum(m_old, jnp.max(s, axis=-1, keepdims=True))
            alpha = jnp.exp(m_old - m_new)
            p = jnp.exp(s - m_new)
            l_sc[...] = alpha * l_sc[...] + jnp.sum(p, axis=-1, keepdims=True)
            acc_sc[...] = alpha * acc_sc[...] + _dot(p.astype(BF16), vc)
            m_sc[...] = m_new
            return carry

        lax.fori_loop(0, n_ch, attn_chunk, 0)
        out = acc_sc[...] / l_sc[...]
        for j in range(GQA_GROUP):
            hh = g * GQA_GROUP + j
            o_ref[:, hh * HEAD_DIM:(hh + 1) * HEAD_DIM] = (
                out[j * Q_TILE:(j + 1) * Q_TILE, :].astype(o_ref.dtype))


def dsa_attention(att_in, w_idx, bsz, seq):
    n = att_in.shape[0]
    nq = seq // Q_TILE
    topk = min(TOPK_MAX, seq // 4)
    kv_blk = ATTN_WIDTH * 2 // KV_WIDTH
    row = lambda b, i: b * nq + i
    return pl.pallas_call(
        functools.partial(_dsa_kernel, topk=topk),
        grid=(bsz, nq),
        in_specs=[pl.BlockSpec((Q_TILE, ATTN_WIDTH), lambda b, i: (row(b, i), 0)),
                  pl.BlockSpec((Q_TILE, IDX_HEADS * IDX_DIM), lambda b, i: (row(b, i), 1)),
                  pl.BlockSpec((seq, KV_WIDTH), lambda b, i: (b, kv_blk)),
                  pl.BlockSpec((seq, KV_WIDTH), lambda b, i: (b, kv_blk + 1)),
                  pl.BlockSpec((seq, 2 * LANES), lambda b, i: (b, kv_blk + 2)),
                  pl.BlockSpec((Q_TILE, LANES), lambda b, i: (row(b, i), 0))],
        out_specs=pl.BlockSpec((Q_TILE, ATTN_WIDTH), lambda b, i: (row(b, i), 0)),
        out_shape=jax.ShapeDtypeStruct((n, ATTN_WIDTH), BF16),
        scratch_shapes=[pltpu.VMEM((seq // K_TILE, Q_TILE, K_TILE), jnp.int32),
                        pltpu.VMEM((GQA_GROUP * Q_TILE, 1), F32),
                        pltpu.VMEM((GQA_GROUP * Q_TILE, 1), F32),
                        pltpu.VMEM((GQA_GROUP * Q_TILE, HEAD_DIM), F32)],
        compiler_params=_params(("parallel", "parallel")),
        name="dsa_attention",
    )(att_in, att_in, att_in, att_in, att_in, w_idx)


def _moe_up_kernel(be_ref, nu_ref, x_ref, w1_ref, w3_ref, o_ref):
    i = pl.program_id(1)

    @pl.when(i < nu_ref[0])
    def _():
        x = x_ref[...]
        o_ref[...] = (jax.nn.silu(_dot(x, w1_ref[...])) * _dot(x, w3_ref[...])).astype(o_ref.dtype)

    @pl.when(i >= nu_ref[0])
    def _():
        o_ref[...] = jnp.zeros_like(o_ref)


def _moe_down_kernel(be_ref, nu_ref, a_ref, w2_ref, o_ref):
    i = pl.program_id(1)

    @pl.when(i < nu_ref[0])
    def _():
        o_ref[...] = _dot(a_ref[...], w2_ref[...])

    @pl.when(i >= nu_ref[0])
    def _():
        o_ref[...] = jnp.zeros_like(o_ref)


def moe_experts(xs, block_e, n_used, w1, w3, w2, bn_up=1024, bn_down=512):
    n_rows, d = xs.shape
    nb = n_rows // MOE_TILE
    f = w1.shape[2]
    blk = lambda i, nu: jnp.minimum(i, nu[0] - 1)
    a = pl.pallas_call(
        _moe_up_kernel,
        grid_spec=pltpu.PrefetchScalarGridSpec(
            num_scalar_prefetch=2,
            grid=(f // bn_up, nb),
            in_specs=[pl.BlockSpec((MOE_TILE, d), lambda j, i, be, nu: (blk(i, nu), 0)),
                      pl.BlockSpec((None, d, bn_up), lambda j, i, be, nu: (be[i], 0, j)),
                      pl.BlockSpec((None, d, bn_up), lambda j, i, be, nu: (be[i], 0, j))],
            out_specs=pl.BlockSpec((MOE_TILE, bn_up), lambda j, i, be, nu: (i, j))),
        out_shape=jax.ShapeDtypeStruct((n_rows, f), BF16),
        compiler_params=_params(("arbitrary", "arbitrary")),
        name="moe_up",
    )(block_e, n_used, xs, w1, w3)
    return pl.pallas_call(
        _moe_down_kernel,
        grid_spec=pltpu.PrefetchScalarGridSpec(
            num_scalar_prefetch=2,
            grid=(d // bn_down, nb),
            in_specs=[pl.BlockSpec((MOE_TILE, f), lambda j, i, be, nu: (blk(i, nu), 0)),
                      pl.BlockSpec((None, f, bn_down), lambda j, i, be, nu: (be[i], 0, j))],
            out_specs=pl.BlockSpec((MOE_TILE, bn_down), lambda j, i, be, nu: (i, j))),
        out_shape=jax.ShapeDtypeStruct((n_rows, d), F32),
        compiler_params=_params(("arbitrary", "arbitrary")),
        name="moe_down",
    )(block_e, n_used, a, w2)


def moe_layer(x, h, rinfo, w1, w3, w2):
    n_tok, d = x.shape
    n_assign = n_tok * TOP_K_EXPERTS
    top_idx = rinfo[:, 0:TOP_K_EXPERTS].astype(jnp.int32)
    gates = rinfo[:, TOP_K_EXPERTS:2 * TOP_K_EXPERTS]
    flat_e = top_idx.reshape(-1)
    onehot = (flat_e[:, None] == jnp.arange(N_EXPERTS, dtype=jnp.int32)[None, :]).astype(jnp.int32)
    csum = jnp.cumsum(onehot, axis=0)
    counts = csum[-1]
    rank = jnp.sum(csum * onehot, axis=1) - 1
    padded = ((counts + MOE_TILE - 1) // MOE_TILE) * MOE_TILE
    ends_padded = jnp.cumsum(padded)
    start_padded = ends_padded - padded
    dest = start_padded[flat_e] + rank
    nb = -(-(n_assign + N_EXPERTS * (MOE_TILE - 1)) // MOE_TILE)
    n_rows = nb * MOE_TILE
    flat_tok = jnp.arange(n_assign, dtype=jnp.int32) // TOP_K_EXPERTS
    row_tok = jnp.zeros((n_rows,), jnp.int32).at[dest].set(flat_tok)
    n_used = (ends_padded[-1] // MOE_TILE).astype(jnp.int32).reshape(1)
    block_e = jnp.searchsorted(ends_padded, jnp.arange(nb, dtype=jnp.int32) * MOE_TILE, side='right')
    last_e = jnp.max(jnp.where(counts > 0, jnp.arange(N_EXPERTS), 0))
    block_e = jnp.minimum(block_e, last_e).astype(jnp.int32)
    xs = h[row_tok]
    y = moe_experts(xs, block_e, n_used, w1, w3, w2)
    dest2 = dest.reshape(n_tok, TOP_K_EXPERTS)
    return x + (y[dest2[:, 0]] * gates[:, 0:1] + y[dest2[:, 1]] * gates[:, 1:2])


def _in_proj_weights(w):
    o = 0
    parts = []
    for width in (CONV_CH, CONV_CH, CONV_CH, ATTN_WIDTH, KV_WIDTH, KV_WIDTH,
                  IDX_HEADS * IDX_DIM, IDX_DIM, IDX_HEADS, 2 * D_MODEL):
        parts.append(w[:, o:o + width])
        o += width
    wb, wc, wu, wq, wk, wv, wqi, wki, wwi, wg = parts
    z = jnp.zeros_like(wki)
    w_att = jnp.concatenate([wq, wqi, wk, wv, wki, z, z, wki], axis=1).astype(BF16)
    w_wi = jnp.pad(wwi, ((0, 0), (0, LANES - IDX_HEADS))).astype(BF16)
    return (wb.astype(BF16), wc.astype(BF16), wu.astype(BF16), w_att, w_wi,
            wg[:, :D_MODEL].astype(BF16), wg[:, D_MODEL:].astype(BF16))


@jax.jit
def _forward(x, mix_norm, w_in, conv_w, w_conv_out, w_attn_out, w_out, ffn_norm,
             ffn_w1, ffn_w3, ffn_w2, router_w, exp_w1, exp_w3, exp_w2, final_norm):
    bsz, seq, d = x.shape
    n = bsz * seq
    depth = w_in.shape[0]
    xf = x.reshape(n, d)
    for layer in range(depth):
        wb, wc, wu, w_att, w_wi, wgc, wga = _in_proj_weights(w_in[layer])
        h = rmsnorm(xf, mix_norm[layer], BF16)
        yc = conv_branch(h, wb, wc, wu, conv_w[layer], seq)
        att_in = matmul(h, w_att, BF16, 1024, w_att.shape[1] // 2, name="attn_in_proj")
        w_idx = matmul(h, w_wi, F32, 1024, LANES, name="idx_weight_proj")
        at = dsa_attention(att_in, w_idx, bsz, seq)
        mix = gated_mix(yc, at, h, w_conv_out[layer].astype(BF16), w_attn_out[layer].astype(BF16),
                        wgc, wga)
        xf = matmul(mix, w_out[layer].astype(BF16), F32, 1024, 1024, residual=xf, name="out_proj")
        j = layer // 2
        if layer % 2 == 0:
            h2 = rmsnorm(xf, ffn_norm[layer], BF16)
            a = swiglu_up(h2, ffn_w1[j].astype(BF16), ffn_w3[j].astype(BF16))
            xf = matmul(a, ffn_w2[j].astype(BF16), F32, 1024, 512, residual=xf, name="ffn_down")
        else:
            wr = jnp.pad(router_w[j], ((0, 0), (0, LANES - N_EXPERTS))).astype(BF16)
            h2, rinfo = rmsnorm_router(xf, ffn_norm[layer], wr)
            xf = moe_layer(xf, h2, rinfo, exp_w1[j].astype(BF16), exp_w3[j].astype(BF16),
                           exp_w2[j].astype(BF16))
    return rmsnorm(xf, final_norm, F32).reshape(bsz, seq, d)


def kernel(x, mix_norm, w_in, conv_w, w_conv_out, w_attn_out, w_out, ffn_norm, ffn_w1, ffn_w3,
           ffn_w2, router_w, exp_w1, exp_w3, exp_w2, final_norm):
    return _forward(x, mix_norm, w_in, conv_w, w_conv_out, w_attn_out, w_out, ffn_norm,
                    ffn_w1, ffn_w3, ffn_w2, router_w, exp_w1, exp_w3, exp_w2, final_norm)
```

```python
import functools

import jax
import jax.numpy as jnp
from jax import lax
from jax.experimental import pallas as pl
from jax.experimental.pallas import tpu as pltpu

BF16 = jnp.bfloat16
F32 = jnp.float32

D_MODEL = 2048
CONV_CH = 1024
N_HEADS = 8
HEAD_DIM = 128
N_KV_HEADS = 2
GQA_GROUP = N_HEADS // N_KV_HEADS
ATTN_WIDTH = N_HEADS * HEAD_DIM
KV_WIDTH = N_KV_HEADS * HEAD_DIM
IDX_HEADS = 16
IDX_DIM = 64
TOPK_MAX = 256
N_EXPERTS = 8
TOP_K_EXPERTS = 2
RMS_EPS = 1e-6

LANES = 128
VMEM_LIMIT = 58 * 1024 * 1024
INT_MIN = -(2 ** 31)
NEG_BIG = -1e30

Q_TILE = 256
K_TILE = 256
COUNT_ROWS = 128
MOE_TILE = 512


def _params(sem):
    return pltpu.CompilerParams(dimension_semantics=sem, vmem_limit_bytes=VMEM_LIMIT)


def _dot(a, b):
    return jnp.dot(a, b, preferred_element_type=F32)


def _dot_nt(a, b):
    return lax.dot_general(a, b, (((1,), (1,)), ((), ())), preferred_element_type=F32)


def _rms(x, g):
    ms = jnp.mean(x * x, axis=-1, keepdims=True)
    return x * lax.rsqrt(ms + RMS_EPS) * g


def _rms_kernel(x_ref, g_ref, o_ref):
    o_ref[...] = _rms(x_ref[...], g_ref[...]).astype(o_ref.dtype)


def rmsnorm(x, g, out_dtype, bm=512):
    n, d = x.shape
    return pl.pallas_call(
        _rms_kernel,
        grid=(n // bm,),
        in_specs=[pl.BlockSpec((bm, d), lambda i: (i, 0)),
                  pl.BlockSpec((1, d), lambda i: (0, 0))],
        out_specs=pl.BlockSpec((bm, d), lambda i: (i, 0)),
        out_shape=jax.ShapeDtypeStruct((n, d), out_dtype),
        compiler_params=_params(("parallel",)),
        name="rmsnorm",
    )(x, g.reshape(1, d))


def _rms_router_kernel(x_ref, g_ref, wr_ref, h_ref, r_ref):
    h = _rms(x_ref[...], g_ref[...]).astype(BF16)
    h_ref[...] = h
    logits = _dot(h, wr_ref[...])
    lane = lax.broadcasted_iota(jnp.int32, logits.shape, 1)
    lane_f = lane.astype(F32)
    lg = jnp.where(lane < N_EXPERTS, logits, -jnp.inf)
    m1 = jnp.max(lg, axis=-1, keepdims=True)
    i1 = jnp.min(jnp.where(lg == m1, lane_f, float(LANES)), axis=-1, keepdims=True)
    lg2 = jnp.where(lane_f == i1, -jnp.inf, lg)
    m2 = jnp.max(lg2, axis=-1, keepdims=True)
    i2 = jnp.min(jnp.where(lg2 == m2, lane_f, float(LANES)), axis=-1, keepdims=True)
    e2 = jnp.exp(m2 - m1)
    den = 1.0 + e2
    g1 = 1.0 / den
    g2 = e2 / den
    r_ref[...] = jnp.where(lane == 0, i1, jnp.where(lane == 1, i2,
                           jnp.where(lane == 2, g1, jnp.where(lane == 3, g2, 0.0))))


def rmsnorm_router(x, g, wr, bm=512):
    n, d = x.shape
    return pl.pallas_call(
        _rms_router_kernel,
        grid=(n // bm,),
        in_specs=[pl.BlockSpec((bm, d), lambda i: (i, 0)),
                  pl.BlockSpec((1, d), lambda i: (0, 0)),
                  pl.BlockSpec((d, LANES), lambda i: (0, 0))],
        out_specs=[pl.BlockSpec((bm, d), lambda i: (i, 0)),
                   pl.BlockSpec((bm, LANES), lambda i: (i, 0))],
        out_shape=[jax.ShapeDtypeStruct((n, d), BF16),
                   jax.ShapeDtypeStruct((n, LANES), F32)],
        compiler_params=_params(("parallel",)),
        name="rmsnorm_router",
    )(x, g.reshape(1, d), wr)


def _mm_kernel(x_ref, w_ref, o_ref):
    o_ref[...] = _dot(x_ref[...], w_ref[...]).astype(o_ref.dtype)


def _mm_res_kernel(x_ref, w_ref, r_ref, o_ref):
    o_ref[...] = (r_ref[...] + _dot(x_ref[...], w_ref[...])).astype(o_ref.dtype)


def matmul(x, w, out_dtype, bm, bn, residual=None, name="matmul"):
    m, k = x.shape
    n = w.shape[1]
    in_specs = [pl.BlockSpec((bm, k), lambda i, j: (i, 0)),
                pl.BlockSpec((k, bn), lambda i, j: (0, j))]
    args = [x, w]
    body = _mm_kernel
    if residual is not None:
        in_specs.append(pl.BlockSpec((bm, bn), lambda i, j: (i, j)))
        args.append(residual)
        body = _mm_res_kernel
    return pl.pallas_call(
        body,
        grid=(m // bm, n // bn),
        in_specs=in_specs,
        out_specs=pl.BlockSpec((bm, bn), lambda i, j: (i, j)),
        out_shape=jax.ShapeDtypeStruct((m, n), out_dtype),
        compiler_params=_params(("parallel", "parallel")),
        name=name,
    )(*args)


def _conv_kernel(h_ref, wb_ref, wc_ref, wu_ref, cw_ref, o_ref):
    h = h_ref[...]
    v = _dot(h, wc_ref[...]) * _dot(h, wu_ref[...])
    row = lax.broadcasted_iota(jnp.int32, v.shape, 0)
    v1 = jnp.where(row >= 1, pltpu.roll(v, 1, 0), 0.0)
    v2 = jnp.where(row >= 2, pltpu.roll(v, 2, 0), 0.0)
    cw = cw_ref[...]
    y = cw[2:3, :] * v + cw[1:2, :] * v1 + cw[0:1, :] * v2
    o_ref[...] = (_dot(h, wb_ref[...]) * y).astype(o_ref.dtype)


def conv_branch(h, wb, wc, wu, conv_w, seq, bn=256):
    n, d = h.shape
    c = wb.shape[1]
    wspec = pl.BlockSpec((d, bn), lambda b, j: (0, j))
    return pl.pallas_call(
        _conv_kernel,
        grid=(n // seq, c // bn),
        in_specs=[pl.BlockSpec((seq, d), lambda b, j: (b, 0)), wspec, wspec, wspec,
                  pl.BlockSpec((conv_w.shape[0], bn), lambda b, j: (0, j))],
        out_specs=pl.BlockSpec((seq, bn), lambda b, j: (b, j)),
        out_shape=jax.ShapeDtypeStruct((n, c), BF16),
        compiler_params=_params(("parallel", "parallel")),
        name="conv_branch",
    )(h, wb, wc, wu, conv_w)


def _mix_kernel(yc_ref, at_ref, h_ref, wco_ref, wao_ref, wgc_ref, wga_ref, o_ref):
    h = h_ref[...]
    g_conv = jax.nn.sigmoid(_dot(h, wgc_ref[...]))
    g_attn = jax.nn.sigmoid(_dot(h, wga_ref[...]))
    y_conv = _dot(yc_ref[...], wco_ref[...])
    y_attn = _dot(at_ref[...], wao_ref[...])
    o_ref[...] = (g_conv * y_conv + g_attn * y_attn).astype(o_ref.dtype)


def gated_mix(yc, at, h, wco, wao, wgc, wga, bm=1024, bn=512):
    n, d = h.shape
    c = yc.shape[1]
    a = at.shape[1]
    return pl.pallas_call(
        _mix_kernel,
        grid=(n // bm, d // bn),
        in_specs=[pl.BlockSpec((bm, c), lambda i, j: (i, 0)),
                  pl.BlockSpec((bm, a), lambda i, j: (i, 0)),
                  pl.BlockSpec((bm, d), lambda i, j: (i, 0)),
                  pl.BlockSpec((c, bn), lambda i, j: (0, j)),
                  pl.BlockSpec((a, bn), lambda i, j: (0, j)),
                  pl.BlockSpec((d, bn), lambda i, j: (0, j)),
                  pl.BlockSpec((d, bn), lambda i, j: (0, j))],
        out_specs=pl.BlockSpec((bm, bn), lambda i, j: (i, j)),
        out_shape=jax.ShapeDtypeStruct((n, d), BF16),
        compiler_params=_params(("parallel", "parallel")),
        name="gated_mix",
    )(yc, at, h, wco, wao, wgc, wga)


def _swiglu_up_kernel(x_ref, w1_ref, w3_ref, o_ref):
    x = x_ref[...]
    o_ref[...] = (jax.nn.silu(_dot(x, w1_ref[...])) * _dot(x, w3_ref[...])).astype(o_ref.dtype)


def swiglu_up(x, w1, w3, bm=1024, bn=512):
    n, d = x.shape
    f = w1.shape[1]
    return pl.pallas_call(
        _swiglu_up_kernel,
        grid=(n // bm, f // bn),
        in_specs=[pl.BlockSpec((bm, d), lambda i, j: (i, 0)),
                  pl.BlockSpec((d, bn), lambda i, j: (0, j)),
                  pl.BlockSpec((d, bn), lambda i, j: (0, j))],
        out_specs=pl.BlockSpec((bm, bn), lambda i, j: (i, j)),
        out_shape=jax.ShapeDtypeStruct((n, f), BF16),
        compiler_params=_params(("parallel", "parallel")),
        name="swiglu_up",
    )(x, w1, w3)


def _dsa_kernel(q_ref, qi_ref, k_ref, v_ref, kbd_ref, wi_ref, o_ref,
                skey_sc, bias_sc, s_sc, m_sc, acc_sc, *, topk):
    qb = pl.program_id(1)
    n_ch = qb + 1
    q_pos = qb * Q_TILE + lax.broadcasted_iota(jnp.int32, (1, Q_TILE), 1)

    w_t = wi_ref[...] * (1.0 / 32.0)
    qi = qi_ref[...]

    def score_chunk(c, carry):
        ks = pl.multiple_of(c * K_TILE, K_TILE)
        k_pair = jnp.concatenate([kbd_ref[pl.ds(ks, K_TILE), 0:LANES],
                                  kbd_ref[pl.ds(ks, K_TILE), LANES:2 * LANES]],
                                 axis=0)
        acc = jnp.zeros((K_TILE, Q_TILE), F32)
        for p in range(IDX_HEADS // 2):
            d = jnp.maximum(_dot_nt(k_pair, qi[:, p * LANES:(p + 1) * LANES]), 0.0)
            acc = acc + w_t[2 * p:2 * p + 1, :] * d[:K_TILE, :]
            acc = acc + w_t[2 * p + 1:2 * p + 2, :] * d[K_TILE:, :]
        key_pos = ks + lax.broadcasted_iota(jnp.int32, (K_TILE, 1), 0)
        bits = pltpu.bitcast(acc, jnp.int32)
        skey = bits ^ ((bits >> 31) & 0x7FFFFFFF)
        skey_sc[c] = jnp.where(key_pos <= q_pos, skey, INT_MIN)
        return carry

    lax.fori_loop(0, n_ch, score_chunk, 0)

    sub = 8

    def count_ge(cand):
        def body(c, part):
            sk = skey_sc[c]
            ge = jnp.where(sk >= cand, 1.0, 0.0)
            for r in range(0, K_TILE, sub):
                part = part + ge[r:r + sub, :]
            return part
        part = lax.fori_loop(0, n_ch, body, jnp.zeros((sub, Q_TILE), F32))
        return jnp.sum(part, axis=0, keepdims=True)

    kf = float(topk)
    zero = jnp.zeros((1, Q_TILE), jnp.int32)
    tau0 = jnp.where(count_ge(zero) >= kf, zero, INT_MIN)

    def bit_step(i, tau):
        cand = tau | lax.shift_left(jnp.int32(1), 30 - i)
        return jnp.where(count_ge(cand) >= kf, cand, tau)

    tau = lax.fori_loop(0, 31, bit_step, tau0)
    tau = jnp.maximum(tau, INT_MIN + 1)

    def bias_chunk(c, carry):
        bias_sc[c] = jnp.where(skey_sc[c] >= tau, 0.0, NEG_BIG).T
        return carry

    lax.fori_loop(0, n_ch, bias_chunk, 0)
    lane_tiles = K_TILE // LANES

    scale = HEAD_DIM ** -0.5
    q = q_ref[...]
    rows = GQA_GROUP * Q_TILE
    for g in range(N_KV_HEADS):
        qg = jnp.concatenate(
            [q[:, (g * GQA_GROUP + j) * HEAD_DIM:(g * GQA_GROUP + j + 1) * HEAD_DIM]
             for j in range(GQA_GROUP)], axis=0)
        m_sc[...] = jnp.full((rows, LANES), NEG_BIG, F32)
        acc_sc[...] = jnp.zeros((rows, 2 * HEAD_DIM), F32)

        def logits_chunk(c, carry):
            ks = pl.multiple_of(c * K_TILE, K_TILE)
            kc = k_ref[pl.ds(ks, K_TILE), g * HEAD_DIM:(g + 1) * HEAD_DIM]
            s = _dot_nt(qg, kc) * scale + jnp.concatenate([bias_sc[c]] * GQA_GROUP, axis=0)
            s_sc[c] = s
            m = m_sc[...]
            for t in range(lane_tiles):
                m = jnp.maximum(m, s[:, t * LANES:(t + 1) * LANES])
            m_sc[...] = m
            return carry

        lax.fori_loop(0, n_ch, logits_chunk, 0)
        m_row = jnp.broadcast_to(jnp.max(m_sc[...], axis=-1, keepdims=True), (rows, LANES))

        def pv_chunk(c, carry):
            ks = pl.multiple_of(c * K_TILE, K_TILE)
            vc = v_ref[pl.ds(ks, K_TILE), g * HEAD_DIM:(g + 1) * HEAD_DIM]
            v_ones = jnp.concatenate([vc, jnp.ones_like(vc)], axis=1)
            s = s_sc[c]
            p = jnp.concatenate([jnp.exp(s[:, t * LANES:(t + 1) * LANES] - m_row)
                                 for t in range(lane_tiles)], axis=1)
            acc_sc[...] += _dot(p.astype(BF16), v_ones)
            return carry

        lax.fori_loop(0, n_ch, pv_chunk, 0)
        acc = acc_sc[...]
        out = acc[:, :HEAD_DIM] / acc[:, HEAD_DIM:]
        for j in range(GQA_GROUP):
            hh = g * GQA_GROUP + j
            o_ref[:, hh * HEAD_DIM:(hh + 1) * HEAD_DIM] = (
                out[j * Q_TILE:(j + 1) * Q_TILE, :].astype(o_ref.dtype))


def dsa_attention(att_in, w_idx, bsz, seq):
    n = att_in.shape[0]
    nq = seq // Q_TILE
    topk = min(TOPK_MAX, seq // 4)
    kv_blk = ATTN_WIDTH * 2 // KV_WIDTH
    row = lambda b, i: b * nq + i
    w_idx_t = w_idx[:, :IDX_HEADS].reshape(n // Q_TILE, Q_TILE, IDX_HEADS).transpose(0, 2, 1)
    return pl.pallas_call(
        functools.partial(_dsa_kernel, topk=topk),
        grid=(bsz, nq),
        in_specs=[pl.BlockSpec((Q_TILE, ATTN_WIDTH), lambda b, i: (row(b, i), 0)),
                  pl.BlockSpec((Q_TILE, IDX_HEADS * IDX_DIM), lambda b, i: (row(b, i), 1)),
                  pl.BlockSpec((seq, KV_WIDTH), lambda b, i: (b, kv_blk)),
                  pl.BlockSpec((seq, KV_WIDTH), lambda b, i: (b, kv_blk + 1)),
                  pl.BlockSpec((seq, 2 * LANES), lambda b, i: (b, kv_blk + 2)),
                  pl.BlockSpec((None, IDX_HEADS, Q_TILE), lambda b, i: (row(b, i), 0, 0))],
        out_specs=pl.BlockSpec((Q_TILE, ATTN_WIDTH), lambda b, i: (row(b, i), 0)),
        out_shape=jax.ShapeDtypeStruct((n, ATTN_WIDTH), BF16),
        scratch_shapes=[pltpu.VMEM((seq // K_TILE, K_TILE, Q_TILE), jnp.int32),
                        pltpu.VMEM((seq // K_TILE, Q_TILE, K_TILE), F32),
                        pltpu.VMEM((seq // K_TILE, GQA_GROUP * Q_TILE, K_TILE), F32),
                        pltpu.VMEM((GQA_GROUP * Q_TILE, LANES), F32),
                        pltpu.VMEM((GQA_GROUP * Q_TILE, 2 * HEAD_DIM), F32)],
        compiler_params=_params(("parallel", "parallel")),
        name="dsa_attention",
    )(att_in, att_in, att_in, att_in, att_in, w_idx_t)


def _moe_up_kernel(be_ref, nu_ref, x_ref, w1_ref, w3_ref, o_ref, w1_sc, w3_sc):
    i = pl.program_id(1)

    @pl.when((i == 0) | (be_ref[i] != be_ref[jnp.maximum(i - 1, 0)]))
    def _():
        w1_sc[...] = w1_ref[...].astype(BF16)
        w3_sc[...] = w3_ref[...].astype(BF16)

    @pl.when(i < nu_ref[0])
    def _():
        x = x_ref[...]
        o_ref[...] = (jax.nn.silu(_dot(x, w1_sc[...])) * _dot(x, w3_sc[...])).astype(o_ref.dtype)

    @pl.when(i >= nu_ref[0])
    def _():
        o_ref[...] = jnp.zeros_like(o_ref)


def _moe_down_kernel(be_ref, nu_ref, a_ref, w2_ref, o_ref):
    i = pl.program_id(1)

    @pl.when(i < nu_ref[0])
    def _():
        o_ref[...] = _dot(a_ref[...], w2_ref[...])

    @pl.when(i >= nu_ref[0])
    def _():
        o_ref[...] = jnp.zeros_like(o_ref)


def moe_experts(xs, block_e, n_used, w1, w3, w2, bn_up=512, bn_down=512):
    n_rows, d = xs.shape
    nb = n_rows // MOE_TILE
    f = w1.shape[2]
    blk = lambda i, nu: jnp.minimum(i, nu[0] - 1)
    a = pl.pallas_call(
        _moe_up_kernel,
        grid_spec=pltpu.PrefetchScalarGridSpec(
            num_scalar_prefetch=2,
            grid=(f // bn_up, nb),
            in_specs=[pl.BlockSpec((MOE_TILE, d), lambda j, i, be, nu: (blk(i, nu), 0)),
                      pl.BlockSpec((None, d, bn_up), lambda j, i, be, nu: (be[i], 0, j)),
                      pl.BlockSpec((None, d, bn_up), lambda j, i, be, nu: (be[i], 0, j))],
            out_specs=pl.BlockSpec((MOE_TILE, bn_up), lambda j, i, be, nu: (i, j)),
            scratch_shapes=[pltpu.VMEM((d, bn_up), BF16), pltpu.VMEM((d, bn_up), BF16)]),
        out_shape=jax.ShapeDtypeStruct((n_rows, f), BF16),
        compiler_params=_params(("arbitrary", "arbitrary")),
        name="moe_up",
    )(block_e, n_used, xs, w1, w3)
    return pl.pallas_call(
        _moe_down_kernel,
        grid_spec=pltpu.PrefetchScalarGridSpec(
            num_scalar_prefetch=2,
            grid=(d // bn_down, nb),
            in_specs=[pl.BlockSpec((MOE_TILE, f), lambda j, i, be, nu: (blk(i, nu), 0)),
                      pl.BlockSpec((None, f, bn_down), lambda j, i, be, nu: (be[i], 0, j))],
            out_specs=pl.BlockSpec((MOE_TILE, bn_down), lambda j, i, be, nu: (i, j))),
        out_shape=jax.ShapeDtypeStruct((n_rows, d), F32),
        compiler_params=_params(("arbitrary", "arbitrary")),
        name="moe_down",
    )(block_e, n_used, a, w2)


def moe_layer(x, h, rinfo, w1, w3, w2):
    n_tok, d = x.shape
    n_assign = n_tok * TOP_K_EXPERTS
    top_idx = rinfo[:, 0:TOP_K_EXPERTS].astype(jnp.int32)
    gates = rinfo[:, TOP_K_EXPERTS:2 * TOP_K_EXPERTS]
    flat_e = top_idx.reshape(-1)
    onehot = (flat_e[:, None] == jnp.arange(N_EXPERTS, dtype=jnp.int32)[None, :]).astype(jnp.int32)
    csum = jnp.cumsum(onehot, axis=0)
    counts = csum[-1]
    rank = jnp.sum(csum * onehot, axis=1) - 1
    padded = ((counts + MOE_TILE - 1) // MOE_TILE) * MOE_TILE
    ends_padded = jnp.cumsum(padded)
    start_padded = ends_padded - padded
    dest = start_padded[flat_e] + rank
    nb = -(-(n_assign + N_EXPERTS * (MOE_TILE - 1)) // MOE_TILE)
    n_rows = nb * MOE_TILE
    flat_tok = jnp.arange(n_assign, dtype=jnp.int32) // TOP_K_EXPERTS
    row_tok = jnp.zeros((n_rows,), jnp.int32).at[dest].set(flat_tok)
    n_used = (ends_padded[-1] // MOE_TILE).astype(jnp.int32).reshape(1)
    block_e = jnp.searchsorted(ends_padded, jnp.arange(nb, dtype=jnp.int32) * MOE_TILE, side='right')
    last_e = jnp.max(jnp.where(counts > 0, jnp.arange(N_EXPERTS), 0))
    block_e = jnp.minimum(block_e, last_e).astype(jnp.int32)
    xs = h[row_tok]
    y = moe_experts(xs, block_e, n_used, w1, w3, w2)
    dest2 = dest.reshape(n_tok, TOP_K_EXPERTS)
    return x + (y[dest2[:, 0]] * gates[:, 0:1] + y[dest2[:, 1]] * gates[:, 1:2])


def _in_proj_weights(w):
    o = 0
    parts = []
    for width in (CONV_CH, CONV_CH, CONV_CH, ATTN_WIDTH, KV_WIDTH, KV_WIDTH,
                  IDX_HEADS * IDX_DIM, IDX_DIM, IDX_HEADS, 2 * D_MODEL):
        parts.append(w[:, o:o + width])
        o += width
    wb, wc, wu, wq, wk, wv, wqi, wki, wwi, wg = parts
    z = jnp.zeros_like(wki)
    w_att = jnp.concatenate([wq, wqi, wk, wv, wki, z, z, wki], axis=1).astype(BF16)
    w_wi = jnp.pad(wwi, ((0, 0), (0, LANES - IDX_HEADS))).astype(BF16)
    return (wb.astype(BF16), wc.astype(BF16), wu.astype(BF16), w_att, w_wi,
            wg[:, :D_MODEL].astype(BF16), wg[:, D_MODEL:].astype(BF16))


@jax.jit
def _forward(x, mix_norm, w_in, conv_w, w_conv_out, w_attn_out, w_out, ffn_norm,
             ffn_w1, ffn_w3, ffn_w2, router_w, exp_w1, exp_w3, exp_w2, final_norm):
    bsz, seq, d = x.shape
    n = bsz * seq
    depth = w_in.shape[0]
    xf = x.reshape(n, d)
    for layer in range(depth):
        wb, wc, wu, w_att, w_wi, wgc, wga = _in_proj_weights(w_in[layer])
        h = rmsnorm(xf, mix_norm[layer], BF16)
        yc = conv_branch(h, wb, wc, wu, conv_w[layer], seq)
        att_in = matmul(h, w_att, BF16, 1024, w_att.shape[1] // 2, name="attn_in_proj")
        w_idx = matmul(h, w_wi, F32, 1024, LANES, name="idx_weight_proj")
        at = dsa_attention(att_in, w_idx, bsz, seq)
        mix = gated_mix(yc, at, h, w_conv_out[layer].astype(BF16), w_attn_out[layer].astype(BF16),
                        wgc, wga)
        xf = matmul(mix, w_out[layer].astype(BF16), F32, 1024, 1024, residual=xf, name="out_proj")
        j = layer // 2
        if layer % 2 == 0:
            h2 = rmsnorm(xf, ffn_norm[layer], BF16)
            a = swiglu_up(h2, ffn_w1[j].astype(BF16), ffn_w3[j].astype(BF16))
            xf = matmul(a, ffn_w2[j].astype(BF16), F32, 1024, 512, residual=xf, name="ffn_down")
        else:
            wr = jnp.pad(router_w[j], ((0, 0), (0, LANES - N_EXPERTS))).astype(BF16)
            h2, rinfo = rmsnorm_router(xf, ffn_norm[layer], wr)
            xf = moe_layer(xf, h2, rinfo, exp_w1[j], exp_w3[j],
                           exp_w2[j].astype(BF16))
    return rmsnorm(xf, final_norm, F32).reshape(bsz, seq, d)


def kernel(x, mix_norm, w_in, conv_w, w_conv_out, w_attn_out, w_out, ffn_norm, ffn_w1, ffn_w3,
           ffn_w2, router_w, exp_w1, exp_w3, exp_w2, final_norm):
    return _forward(x, mix_norm, w_in, conv_w, w_conv_out, w_attn_out, w_out, ffn_norm,
                    ffn_w1, ffn_w3, ffn_w2, router_w, exp_w1, exp_w3, exp_w2, final_norm)
```

```python
import functools

import jax
import jax.numpy as jnp
from jax import lax
from jax.experimental import pallas as pl
from jax.experimental.pallas import tpu as pltpu

BF16 = jnp.bfloat16
F32 = jnp.float32

D_MODEL = 2048
CONV_CH = 1024
N_HEADS = 8
HEAD_DIM = 128
N_KV_HEADS = 2
GQA_GROUP = N_HEADS // N_KV_HEADS
ATTN_WIDTH = N_HEADS * HEAD_DIM
KV_WIDTH = N_KV_HEADS * HEAD_DIM
IDX_HEADS = 16
IDX_DIM = 64
TOPK_MAX = 256
N_EXPERTS = 8
TOP_K_EXPERTS = 2
RMS_EPS = 1e-6

LANES = 128
VMEM_LIMIT = 58 * 1024 * 1024
INT_MIN = -(2 ** 31)
NEG_BIG = -1e30

Q_TILE = 256
K_TILE = 256
COUNT_ROWS = 128
MOE_TILE = 512


def _params(sem):
    return pltpu.CompilerParams(dimension_semantics=sem, vmem_limit_bytes=VMEM_LIMIT)


def _dot(a, b):
    return jnp.dot(a, b, preferred_element_type=F32)


def _dot_nt(a, b):
    return lax.dot_general(a, b, (((1,), (1,)), ((), ())), preferred_element_type=F32)


def _rms(x, g):
    ms = jnp.mean(x * x, axis=-1, keepdims=True)
    return x * lax.rsqrt(ms + RMS_EPS) * g


def _rms_kernel(x_ref, g_ref, o_ref):
    o_ref[...] = _rms(x_ref[...], g_ref[...]).astype(o_ref.dtype)


def rmsnorm(x, g, out_dtype, bm=512):
    n, d = x.shape
    return pl.pallas_call(
        _rms_kernel,
        grid=(n // bm,),
        in_specs=[pl.BlockSpec((bm, d), lambda i: (i, 0)),
                  pl.BlockSpec((1, d), lambda i: (0, 0))],
        out_specs=pl.BlockSpec((bm, d), lambda i: (i, 0)),
        out_shape=jax.ShapeDtypeStruct((n, d), out_dtype),
        compiler_params=_params(("parallel",)),
        name="rmsnorm",
    )(x, g.reshape(1, d))


def _rms_router_kernel(x_ref, g_ref, wr_ref, h_ref, r_ref, cnt_ref, run_sc):
    @pl.when(pl.program_id(0) == 0)
    def _():
        run_sc[...] = jnp.zeros_like(run_sc)

    hf = _rms(x_ref[...], g_ref[...])
    h_ref[...] = hf
    logits = _dot(hf.astype(BF16), wr_ref[...])
    lane = lax.broadcasted_iota(jnp.int32, logits.shape, 1)
    lane_f = lane.astype(F32)
    lg = jnp.where(lane < N_EXPERTS, logits, -jnp.inf)
    m1 = jnp.max(lg, axis=-1, keepdims=True)
    i1 = jnp.min(jnp.where(lg == m1, lane_f, float(LANES)), axis=-1, keepdims=True)
    lg2 = jnp.where(lane_f == i1, -jnp.inf, lg)
    m2 = jnp.max(lg2, axis=-1, keepdims=True)
    i2 = jnp.min(jnp.where(lg2 == m2, lane_f, float(LANES)), axis=-1, keepdims=True)
    e2 = jnp.exp(m2 - m1)
    den = 1.0 + e2
    g1 = 1.0 / den
    g2 = e2 / den
    oh1 = jnp.where(lane_f == i1, 1.0, 0.0)
    oh2 = jnp.where(lane_f == i2, 1.0, 0.0)
    both = oh1 + oh2
    bm = both.shape[0]
    earlier = (lax.broadcasted_iota(jnp.int32, (bm, bm), 0) >
               lax.broadcasted_iota(jnp.int32, (bm, bm), 1))
    prefix = _dot(jnp.where(earlier, 1.0, 0.0).astype(BF16), both.astype(BF16)) + run_sc[0:1, :]
    s1 = jnp.sum(oh1 * prefix, axis=-1, keepdims=True)
    s2 = jnp.sum(oh2 * prefix, axis=-1, keepdims=True)
    run_sc[...] = run_sc[...] + jnp.sum(both, axis=0, keepdims=True)
    cnt_ref[...] = run_sc[...]
    r_ref[...] = jnp.where(lane == 0, i1, jnp.where(lane == 1, i2,
                           jnp.where(lane == 2, g1, jnp.where(lane == 3, g2,
                                     jnp.where(lane == 4, s1, jnp.where(lane == 5, s2, 0.0))))))


def rmsnorm_router(x, g, wr, bm=512):
    n, d = x.shape
    sub = 8
    return pl.pallas_call(
        _rms_router_kernel,
        grid=(n // bm,),
        in_specs=[pl.BlockSpec((bm, d), lambda i: (i, 0)),
                  pl.BlockSpec((1, d), lambda i: (0, 0)),
                  pl.BlockSpec((d, LANES), lambda i: (0, 0))],
        out_specs=[pl.BlockSpec((bm, d), lambda i: (i, 0)),
                   pl.BlockSpec((bm, LANES), lambda i: (i, 0)),
                   pl.BlockSpec((sub, LANES), lambda i: (0, 0))],
        out_shape=[jax.ShapeDtypeStruct((n, d), F32),
                   jax.ShapeDtypeStruct((n, LANES), F32),
                   jax.ShapeDtypeStruct((sub, LANES), F32)],
        scratch_shapes=[pltpu.VMEM((sub, LANES), F32)],
        compiler_params=_params(("arbitrary",)),
        name="rmsnorm_router",
    )(x, g.reshape(1, d), wr)


def _mm_kernel(x_ref, w_ref, o_ref):
    o_ref[...] = _dot(x_ref[...], w_ref[...]).astype(o_ref.dtype)


def _mm_res_kernel(x_ref, w_ref, r_ref, o_ref):
    o_ref[...] = (r_ref[...] + _dot(x_ref[...], w_ref[...])).astype(o_ref.dtype)


def matmul(x, w, out_dtype, bm, bn, residual=None, name="matmul"):
    m, k = x.shape
    n = w.shape[1]
    in_specs = [pl.BlockSpec((bm, k), lambda i, j: (i, 0)),
                pl.BlockSpec((k, bn), lambda i, j: (0, j))]
    args = [x, w]
    body = _mm_kernel
    if residual is not None:
        in_specs.append(pl.BlockSpec((bm, bn), lambda i, j: (i, j)))
        args.append(residual)
        body = _mm_res_kernel
    return pl.pallas_call(
        body,
        grid=(m // bm, n // bn),
        in_specs=in_specs,
        out_specs=pl.BlockSpec((bm, bn), lambda i, j: (i, j)),
        out_shape=jax.ShapeDtypeStruct((m, n), out_dtype),
        compiler_params=_params(("parallel", "parallel")),
        name=name,
    )(*args)


def _conv_kernel(h_ref, wb_ref, wc_ref, wu_ref, cw_ref, o_ref):
    h = h_ref[...]
    v = _dot(h, wc_ref[...]) * _dot(h, wu_ref[...])
    row = lax.broadcasted_iota(jnp.int32, v.shape, 0)
    v1 = jnp.where(row >= 1, pltpu.roll(v, 1, 0), 0.0)
    v2 = jnp.where(row >= 2, pltpu.roll(v, 2, 0), 0.0)
    cw = cw_ref[...]
    y = cw[2:3, :] * v + cw[1:2, :] * v1 + cw[0:1, :] * v2
    o_ref[...] = (_dot(h, wb_ref[...]) * y).astype(o_ref.dtype)


def conv_branch(h, wb, wc, wu, conv_w, seq, bn=256):
    n, d = h.shape
    c = wb.shape[1]
    wspec = pl.BlockSpec((d, bn), lambda b, j: (0, j))
    return pl.pallas_call(
        _conv_kernel,
        grid=(n // seq, c // bn),
        in_specs=[pl.BlockSpec((seq, d), lambda b, j: (b, 0)), wspec, wspec, wspec,
                  pl.BlockSpec((conv_w.shape[0], bn), lambda b, j: (0, j))],
        out_specs=pl.BlockSpec((seq, bn), lambda b, j: (b, j)),
        out_shape=jax.ShapeDtypeStruct((n, c), BF16),
        compiler_params=_params(("parallel", "parallel")),
        name="conv_branch",
    )(h, wb, wc, wu, conv_w)


def _mix_kernel(yc_ref, at_ref, h_ref, wco_ref, wao_ref, wgc_ref, wga_ref, o_ref):
    h = h_ref[...]
    g_conv = jax.nn.sigmoid(_dot(h, wgc_ref[...]))
    g_attn = jax.nn.sigmoid(_dot(h, wga_ref[...]))
    y_conv = _dot(yc_ref[...], wco_ref[...])
    y_attn = _dot(at_ref[...], wao_ref[...])
    o_ref[...] = (g_conv * y_conv + g_attn * y_attn).astype(o_ref.dtype)


def gated_mix(yc, at, h, wco, wao, wgc, wga, bm=1024, bn=512):
    n, d = h.shape
    c = yc.shape[1]
    a = at.shape[1]
    return pl.pallas_call(
        _mix_kernel,
        grid=(n // bm, d // bn),
        in_specs=[pl.BlockSpec((bm, c), lambda i, j: (i, 0)),
                  pl.BlockSpec((bm, a), lambda i, j: (i, 0)),
                  pl.BlockSpec((bm, d), lambda i, j: (i, 0)),
                  pl.BlockSpec((c, bn), lambda i, j: (0, j)),
                  pl.BlockSpec((a, bn), lambda i, j: (0, j)),
                  pl.BlockSpec((d, bn), lambda i, j: (0, j)),
                  pl.BlockSpec((d, bn), lambda i, j: (0, j))],
        out_specs=pl.BlockSpec((bm, bn), lambda i, j: (i, j)),
        out_shape=jax.ShapeDtypeStruct((n, d), BF16),
        compiler_params=_params(("parallel", "parallel")),
        name="gated_mix",
    )(yc, at, h, wco, wao, wgc, wga)


def _swiglu_up_kernel(x_ref, w1_ref, w3_ref, o_ref):
    x = x_ref[...]
    o_ref[...] = (jax.nn.silu(_dot(x, w1_ref[...])) * _dot(x, w3_ref[...])).astype(o_ref.dtype)


def swiglu_up(x, w1, w3, bm=1024, bn=512):
    n, d = x.shape
    f = w1.shape[1]
    return pl.pallas_call(
        _swiglu_up_kernel,
        grid=(n // bm, f // bn),
        in_specs=[pl.BlockSpec((bm, d), lambda i, j: (i, 0)),
                  pl.BlockSpec((d, bn), lambda i, j: (0, j)),
                  pl.BlockSpec((d, bn), lambda i, j: (0, j))],
        out_specs=pl.BlockSpec((bm, bn), lambda i, j: (i, j)),
        out_shape=jax.ShapeDtypeStruct((n, f), BF16),
        compiler_params=_params(("parallel", "parallel")),
        name="swiglu_up",
    )(x, w1, w3)


def _dsa_kernel(q_ref, qi_ref, k_ref, v_ref, kbd_ref, wi_ref, o_ref,
                skey_sc, bias_sc, s_sc, m_sc, acc_sc, *, topk):
    qb = pl.program_id(1)
    n_ch = qb + 1
    q_pos = qb * Q_TILE + lax.broadcasted_iota(jnp.int32, (1, Q_TILE), 1)

    w_t = wi_ref[...] * (1.0 / 32.0)
    qi = qi_ref[...]

    def score_chunk(c, carry):
        ks = pl.multiple_of(c * K_TILE, K_TILE)
        k_pair = jnp.concatenate([kbd_ref[pl.ds(ks, K_TILE), 0:LANES],
                                  kbd_ref[pl.ds(ks, K_TILE), LANES:2 * LANES]],
                                 axis=0)
        acc = jnp.zeros((K_TILE, Q_TILE), F32)
        for p in range(IDX_HEADS // 2):
            d = jnp.maximum(_dot_nt(k_pair, qi[:, p * LANES:(p + 1) * LANES]), 0.0)
            acc = acc + w_t[2 * p:2 * p + 1, :] * d[:K_TILE, :]
            acc = acc + w_t[2 * p + 1:2 * p + 2, :] * d[K_TILE:, :]
        key_pos = ks + lax.broadcasted_iota(jnp.int32, (K_TILE, 1), 0)
        bits = pltpu.bitcast(acc, jnp.int32)
        skey = bits ^ ((bits >> 31) & 0x7FFFFFFF)
        skey_sc[c] = jnp.where(key_pos <= q_pos, skey, INT_MIN)
        return carry

    lax.fori_loop(0, n_ch, score_chunk, 0)

    sub = 8

    def count_ge(cand):
        def body(c, part):
            sk = skey_sc[c]
            ge = jnp.where(sk >= cand, 1.0, 0.0)
            for r in range(0, K_TILE, sub):
                part = part + ge[r:r + sub, :]
            return part
        part = lax.fori_loop(0, n_ch, body, jnp.zeros((sub, Q_TILE), F32))
        return jnp.sum(part, axis=0, keepdims=True)

    kf = float(topk)
    zero = jnp.zeros((1, Q_TILE), jnp.int32)
    tau0 = jnp.where(count_ge(zero) >= kf, zero, INT_MIN)

    def bit_step(i, tau):
        cand = tau | lax.shift_left(jnp.int32(1), 30 - i)
        return jnp.where(count_ge(cand) >= kf, cand, tau)

    tau = lax.fori_loop(0, 31, bit_step, tau0)
    tau = jnp.maximum(tau, INT_MIN + 1)

    def bias_chunk(c, carry):
        bias_sc[c] = jnp.where(skey_sc[c] >= tau, 0.0, NEG_BIG).T
        return carry

    lax.fori_loop(0, n_ch, bias_chunk, 0)
    lane_tiles = K_TILE // LANES

    scale = HEAD_DIM ** -0.5
    q = q_ref[...]
    rows = GQA_GROUP * Q_TILE
    for g in range(N_KV_HEADS):
        qg = jnp.concatenate(
            [q[:, (g * GQA_GROUP + j) * HEAD_DIM:(g * GQA_GROUP + j + 1) * HEAD_DIM]
             for j in range(GQA_GROUP)], axis=0)
        m_sc[...] = jnp.full((rows, LANES), NEG_BIG, F32)
        acc_sc[...] = jnp.zeros((rows, 2 * HEAD_DIM), F32)

        def logits_chunk(c, carry):
            ks = pl.multiple_of(c * K_TILE, K_TILE)
            kc = k_ref[pl.ds(ks, K_TILE), g * HEAD_DIM:(g + 1) * HEAD_DIM]
            s = _dot_nt(qg, kc) * scale + jnp.concatenate([bias_sc[c]] * GQA_GROUP, axis=0)
            s_sc[c] = s
            m = m_sc[...]
            for t in range(lane_tiles):
                m = jnp.maximum(m, s[:, t * LANES:(t + 1) * LANES])
            m_sc[...] = m
            return carry

        lax.fori_loop(0, n_ch, logits_chunk, 0)
        m_row = jnp.broadcast_to(jnp.max(m_sc[...], axis=-1, keepdims=True), (rows, LANES))

        def pv_chunk(c, carry):
            ks = pl.multiple_of(c * K_TILE, K_TILE)
            vc = v_ref[pl.ds(ks, K_TILE), g * HEAD_DIM:(g + 1) * HEAD_DIM]
            v_ones = jnp.concatenate([vc, jnp.ones_like(vc)], axis=1)
            s = s_sc[c]
            p = jnp.concatenate([jnp.exp(s[:, t * LANES:(t + 1) * LANES] - m_row)
                                 for t in range(lane_tiles)], axis=1)
            acc_sc[...] += _dot(p.astype(BF16), v_ones)
            return carry

        lax.fori_loop(0, n_ch, pv_chunk, 0)
        acc = acc_sc[...]
        out = acc[:, :HEAD_DIM] / acc[:, HEAD_DIM:]
        for j in range(GQA_GROUP):
            hh = g * GQA_GROUP + j
            o_ref[:, hh * HEAD_DIM:(hh + 1) * HEAD_DIM] = (
                out[j * Q_TILE:(j + 1) * Q_TILE, :].astype(o_ref.dtype))


def dsa_attention(att_in, w_idx, bsz, seq):
    n = att_in.shape[0]
    nq = seq // Q_TILE
    topk = min(TOPK_MAX, seq // 4)
    kv_blk = ATTN_WIDTH * 2 // KV_WIDTH
    row = lambda b, i: b * nq + i
    w_idx_t = w_idx[:, :IDX_HEADS].reshape(n // Q_TILE, Q_TILE, IDX_HEADS).transpose(0, 2, 1)
    return pl.pallas_call(
        functools.partial(_dsa_kernel, topk=topk),
        grid=(bsz, nq),
        in_specs=[pl.BlockSpec((Q_TILE, ATTN_WIDTH), lambda b, i: (row(b, i), 0)),
                  pl.BlockSpec((Q_TILE, IDX_HEADS * IDX_DIM), lambda b, i: (row(b, i), 1)),
                  pl.BlockSpec((seq, KV_WIDTH), lambda b, i: (b, kv_blk)),
                  pl.BlockSpec((seq, KV_WIDTH), lambda b, i: (b, kv_blk + 1)),
                  pl.BlockSpec((seq, 2 * LANES), lambda b, i: (b, kv_blk + 2)),
                  pl.BlockSpec((None, IDX_HEADS, Q_TILE), lambda b, i: (row(b, i), 0, 0))],
        out_specs=pl.BlockSpec((Q_TILE, ATTN_WIDTH), lambda b, i: (row(b, i), 0)),
        out_shape=jax.ShapeDtypeStruct((n, ATTN_WIDTH), BF16),
        scratch_shapes=[pltpu.VMEM((seq // K_TILE, K_TILE, Q_TILE), jnp.int32),
                        pltpu.VMEM((seq // K_TILE, Q_TILE, K_TILE), F32),
                        pltpu.VMEM((seq // K_TILE, GQA_GROUP * Q_TILE, K_TILE), F32),
                        pltpu.VMEM((GQA_GROUP * Q_TILE, LANES), F32),
                        pltpu.VMEM((GQA_GROUP * Q_TILE, 2 * HEAD_DIM), F32)],
        compiler_params=_params(("parallel", "parallel")),
        name="dsa_attention",
    )(att_in, att_in, att_in, att_in, att_in, w_idx_t)


def _row_copy(src_ref, src_row, dst_ref, dst_row, sem):
    return pltpu.make_async_copy(src_ref.at[pl.ds(src_row, 1), :], dst_ref.at[pl.ds(dst_row, 1), :], sem)


def _dispatch_kernel(dest_ref, h_hbm, xs_in, xs_out, sem, *, bt):
    del xs_in
    base = pl.program_id(0) * bt

    def start(r, carry):
        for k in range(TOP_K_EXPERTS):
            _row_copy(h_hbm, base + r, xs_out, dest_ref[r * TOP_K_EXPERTS + k], sem).start()
        return carry

    def wait(r, carry):
        for k in range(TOP_K_EXPERTS):
            _row_copy(h_hbm, base + r, xs_out, dest_ref[r * TOP_K_EXPERTS + k], sem).wait()
        return carry

    lax.fori_loop(0, bt, start, 0)
    lax.fori_loop(0, bt, wait, 0)


def moe_dispatch(h, dest, n_rows, bt=512):
    n, d = h.shape
    xs0 = jnp.zeros((n_rows, d), h.dtype)
    return pl.pallas_call(
        functools.partial(_dispatch_kernel, bt=bt),
        grid=(n // bt,),
        in_specs=[pl.BlockSpec((bt * TOP_K_EXPERTS,), lambda i: (i,), memory_space=pltpu.SMEM),
                  pl.BlockSpec(memory_space=pl.ANY),
                  pl.BlockSpec(memory_space=pl.ANY)],
        out_specs=pl.BlockSpec(memory_space=pl.ANY),
        out_shape=jax.ShapeDtypeStruct((n_rows, d), h.dtype),
        scratch_shapes=[pltpu.SemaphoreType.DMA(())],
        input_output_aliases={2: 0},
        compiler_params=pltpu.CompilerParams(dimension_semantics=("arbitrary",),
                                             has_side_effects=True),
        name="moe_dispatch",
    )(dest.reshape(-1), h, xs0)


def _combine_kernel(dest_ref, x_ref, r_ref, y_hbm, o_ref, buf, sem, *, bt):
    def start(r, carry):
        for k in range(TOP_K_EXPERTS):
            _row_copy(y_hbm, dest_ref[r * TOP_K_EXPERTS + k], buf.at[k], r, sem).start()
        return carry

    def wait(r, carry):
        for k in range(TOP_K_EXPERTS):
            _row_copy(y_hbm, dest_ref[r * TOP_K_EXPERTS + k], buf.at[k], r, sem).wait()
        return carry

    lax.fori_loop(0, bt, start, 0)
    lax.fori_loop(0, bt, wait, 0)
    r = r_ref[...]
    o_ref[...] = x_ref[...] + (buf[0] * r[:, 2:3] + buf[1] * r[:, 3:4])


def moe_combine(x, rinfo, y, dest, bt=512):
    n, d = x.shape
    return pl.pallas_call(
        functools.partial(_combine_kernel, bt=bt),
        grid=(n // bt,),
        in_specs=[pl.BlockSpec((bt * TOP_K_EXPERTS,), lambda i: (i,), memory_space=pltpu.SMEM),
                  pl.BlockSpec((bt, d), lambda i: (i, 0)),
                  pl.BlockSpec((bt, LANES), lambda i: (i, 0)),
                  pl.BlockSpec(memory_space=pl.ANY)],
        out_specs=pl.BlockSpec((bt, d), lambda i: (i, 0)),
        out_shape=jax.ShapeDtypeStruct((n, d), x.dtype),
        scratch_shapes=[pltpu.VMEM((TOP_K_EXPERTS, bt, d), y.dtype), pltpu.SemaphoreType.DMA(())],
        compiler_params=_params(("arbitrary",)),
        name="moe_combine",
    )(dest.reshape(-1), x, rinfo, y)


def _moe_up_kernel(be_ref, nu_ref, x_ref, w1_ref, w3_ref, o_ref):
    i = pl.program_id(1)

    @pl.when(i < nu_ref[0])
    def _():
        x = x_ref[...].astype(BF16)
        o_ref[...] = (jax.nn.silu(_dot(x, w1_ref[...])) * _dot(x, w3_ref[...])).astype(o_ref.dtype)

    @pl.when(i >= nu_ref[0])
    def _():
        o_ref[...] = jnp.zeros_like(o_ref)


def _moe_down_kernel(be_ref, nu_ref, a_ref, w2_ref, o_ref):
    i = pl.program_id(1)

    @pl.when(i < nu_ref[0])
    def _():
        o_ref[...] = _dot(a_ref[...], w2_ref[...])

    @pl.when(i >= nu_ref[0])
    def _():
        o_ref[...] = jnp.zeros_like(o_ref)


def moe_experts(xs, block_e, n_used, w1, w3, w2, bn_up=1024, bn_down=512):
    n_rows, d = xs.shape
    nb = n_rows // MOE_TILE
    f = w1.shape[2]
    blk = lambda i, nu: jnp.minimum(i, nu[0] - 1)
    a = pl.pallas_call(
        _moe_up_kernel,
        grid_spec=pltpu.PrefetchScalarGridSpec(
            num_scalar_prefetch=2,
            grid=(f // bn_up, nb),
            in_specs=[pl.BlockSpec((MOE_TILE, d), lambda j, i, be, nu: (blk(i, nu), 0)),
                      pl.BlockSpec((None, d, bn_up), lambda j, i, be, nu: (be[i], 0, j)),
                      pl.BlockSpec((None, d, bn_up), lambda j, i, be, nu: (be[i], 0, j))],
            out_specs=pl.BlockSpec((MOE_TILE, bn_up), lambda j, i, be, nu: (i, j))),
        out_shape=jax.ShapeDtypeStruct((n_rows, f), BF16),
        compiler_params=_params(("arbitrary", "arbitrary")),
        name="moe_up",
    )(block_e, n_used, xs, w1, w3)
    return pl.pallas_call(
        _moe_down_kernel,
        grid_spec=pltpu.PrefetchScalarGridSpec(
            num_scalar_prefetch=2,
            grid=(d // bn_down, nb),
            in_specs=[pl.BlockSpec((MOE_TILE, f), lambda j, i, be, nu: (blk(i, nu), 0)),
                      pl.BlockSpec((None, f, bn_down), lambda j, i, be, nu: (be[i], 0, j))],
            out_specs=pl.BlockSpec((MOE_TILE, bn_down), lambda j, i, be, nu: (i, j))),
        out_shape=jax.ShapeDtypeStruct((n_rows, d), F32),
        compiler_params=_params(("arbitrary", "arbitrary")),
        name="moe_down",
    )(block_e, n_used, a, w2)


def moe_layer(x, h, rinfo, counts_f, w1, w3, w2):
    n_tok, d = x.shape
    n_assign = n_tok * TOP_K_EXPERTS
    top_idx = rinfo[:, 0:TOP_K_EXPERTS].astype(jnp.int32)
    slot = rinfo[:, 2 * TOP_K_EXPERTS:3 * TOP_K_EXPERTS].astype(jnp.int32)
    counts = counts_f[0, :N_EXPERTS].astype(jnp.int32)
    padded = ((counts + MOE_TILE - 1) // MOE_TILE) * MOE_TILE
    ends_padded = jnp.cumsum(padded)
    start_padded = ends_padded - padded
    dest = start_padded[top_idx] + slot
    nb = -(-(n_assign + N_EXPERTS * (MOE_TILE - 1)) // MOE_TILE)
    n_used = (ends_padded[-1] // MOE_TILE).astype(jnp.int32).reshape(1)
    block_e = jnp.searchsorted(ends_padded, jnp.arange(nb, dtype=jnp.int32) * MOE_TILE, side='right')
    last_e = jnp.max(jnp.where(counts > 0, jnp.arange(N_EXPERTS), 0))
    block_e = jnp.minimum(block_e, last_e).astype(jnp.int32)
    xs = moe_dispatch(h, dest, nb * MOE_TILE)
    y = moe_experts(xs, block_e, n_used, w1, w3, w2)
    return moe_combine(x, rinfo, y, dest)


def _in_proj_weights(w):
    o = 0
    parts = []
    for width in (CONV_CH, CONV_CH, CONV_CH, ATTN_WIDTH, KV_WIDTH, KV_WIDTH,
                  IDX_HEADS * IDX_DIM, IDX_DIM, IDX_HEADS, 2 * D_MODEL):
        parts.append(w[:, o:o + width])
        o += width
    wb, wc, wu, wq, wk, wv, wqi, wki, wwi, wg = parts
    z = jnp.zeros_like(wki)
    w_att = jnp.concatenate([wq, wqi, wk, wv, wki, z, z, wki], axis=1).astype(BF16)
    w_wi = jnp.pad(wwi, ((0, 0), (0, LANES - IDX_HEADS))).astype(BF16)
    return (wb.astype(BF16), wc.astype(BF16), wu.astype(BF16), w_att, w_wi,
            wg[:, :D_MODEL].astype(BF16), wg[:, D_MODEL:].astype(BF16))


@jax.jit
def _forward(x, mix_norm, w_in, conv_w, w_conv_out, w_attn_out, w_out, ffn_norm,
             ffn_w1, ffn_w3, ffn_w2, router_w, exp_w1, exp_w3, exp_w2, final_norm):
    bsz, seq, d = x.shape
    n = bsz * seq
    depth = w_in.shape[0]
    xf = x.reshape(n, d)
    for layer in range(depth):
        wb, wc, wu, w_att, w_wi, wgc, wga = _in_proj_weights(w_in[layer])
        h = rmsnorm(xf, mix_norm[layer], BF16)
        yc = conv_branch(h, wb, wc, wu, conv_w[layer], seq)
        att_in = matmul(h, w_att, BF16, 1024, w_att.shape[1] // 2, name="attn_in_proj")
        w_idx = matmul(h, w_wi, F32, 1024, LANES, name="idx_weight_proj")
        at = dsa_attention(att_in, w_idx, bsz, seq)
        mix = gated_mix(yc, at, h, w_conv_out[layer].astype(BF16), w_attn_out[layer].astype(BF16),
                        wgc, wga)
        xf = matmul(mix, w_out[layer].astype(BF16), F32, 1024, 1024, residual=xf, name="out_proj")
        j = layer // 2
        if layer % 2 == 0:
            h2 = rmsnorm(xf, ffn_norm[layer], BF16)
            a = swiglu_up(h2, ffn_w1[j].astype(BF16), ffn_w3[j].astype(BF16))
            xf = matmul(a, ffn_w2[j].astype(BF16), F32, 1024, 512, residual=xf, name="ffn_down")
        else:
            wr = jnp.pad(router_w[j], ((0, 0), (0, LANES - N_EXPERTS))).astype(BF16)
            h2, rinfo, counts = rmsnorm_router(xf, ffn_norm[layer], wr)
            xf = moe_layer(xf, h2, rinfo, counts, exp_w1[j].astype(BF16), exp_w3[j].astype(BF16),
                           exp_w2[j].astype(BF16))
    return rmsnorm(xf, final_norm, F32).reshape(bsz, seq, d)


def kernel(x, mix_norm, w_in, conv_w, w_conv_out, w_attn_out, w_out, ffn_norm, ffn_w1, ffn_w3,
           ffn_w2, router_w, exp_w1, exp_w3, exp_w2, final_norm):
    return _forward(x, mix_norm, w_in, conv_w, w_conv_out, w_attn_out, w_out, ffn_norm,
                    ffn_w1, ffn_w3, ffn_w2, router_w, exp_w1, exp_w3, exp_w2, final_norm)
```

```python
import functools

import jax
import jax.numpy as jnp
from jax import lax
from jax.experimental import pallas as pl
from jax.experimental.pallas import tpu as pltpu

BF16 = jnp.bfloat16
F32 = jnp.float32

D_MODEL = 2048
CONV_CH = 1024
N_HEADS = 8
HEAD_DIM = 128
N_KV_HEADS = 2
GQA_GROUP = N_HEADS // N_KV_HEADS
ATTN_WIDTH = N_HEADS * HEAD_DIM
KV_WIDTH = N_KV_HEADS * HEAD_DIM
IDX_HEADS = 16
IDX_DIM = 64
TOPK_MAX = 256
N_EXPERTS = 8
TOP_K_EXPERTS = 2
RMS_EPS = 1e-6

LANES = 128
VMEM_LIMIT = 58 * 1024 * 1024
INT_MIN = -(2 ** 31)
NEG_BIG = -1e30

Q_TILE = 256
K_TILE = 256
COUNT_ROWS = 128
MOE_TILE = 512


def _params(sem):
    return pltpu.CompilerParams(dimension_semantics=sem, vmem_limit_bytes=VMEM_LIMIT)


def _dot(a, b):
    return jnp.dot(a, b, preferred_element_type=F32)


def _dot_nt(a, b):
    return lax.dot_general(a, b, (((1,), (1,)), ((), ())), preferred_element_type=F32)


def _rms(x, g):
    ms = jnp.mean(x * x, axis=-1, keepdims=True)
    return x * lax.rsqrt(ms + RMS_EPS) * g


def _rms_kernel(x_ref, g_ref, o_ref):
    o_ref[...] = _rms(x_ref[...], g_ref[...]).astype(o_ref.dtype)


def rmsnorm(x, g, out_dtype, bm=512):
    n, d = x.shape
    return pl.pallas_call(
        _rms_kernel,
        grid=(n // bm,),
        in_specs=[pl.BlockSpec((bm, d), lambda i: (i, 0)),
                  pl.BlockSpec((1, d), lambda i: (0, 0))],
        out_specs=pl.BlockSpec((bm, d), lambda i: (i, 0)),
        out_shape=jax.ShapeDtypeStruct((n, d), out_dtype),
        compiler_params=_params(("parallel",)),
        name="rmsnorm",
    )(x, g.reshape(1, d))


def _rms_router_kernel(x_ref, g_ref, wr_ref, h_ref, r_ref, cnt_ref, run_sc):
    @pl.when(pl.program_id(0) == 0)
    def _():
        run_sc[...] = jnp.zeros_like(run_sc)

    hf = _rms(x_ref[...], g_ref[...])
    h_ref[...] = hf
    logits = _dot(hf.astype(BF16), wr_ref[...])
    lane = lax.broadcasted_iota(jnp.int32, logits.shape, 1)
    lane_f = lane.astype(F32)
    lg = jnp.where(lane < N_EXPERTS, logits, -jnp.inf)
    m1 = jnp.max(lg, axis=-1, keepdims=True)
    i1 = jnp.min(jnp.where(lg == m1, lane_f, float(LANES)), axis=-1, keepdims=True)
    lg2 = jnp.where(lane_f == i1, -jnp.inf, lg)
    m2 = jnp.max(lg2, axis=-1, keepdims=True)
    i2 = jnp.min(jnp.where(lg2 == m2, lane_f, float(LANES)), axis=-1, keepdims=True)
    e2 = jnp.exp(m2 - m1)
    den = 1.0 + e2
    g1 = 1.0 / den
    g2 = e2 / den
    oh1 = jnp.where(lane_f == i1, 1.0, 0.0)
    oh2 = jnp.where(lane_f == i2, 1.0, 0.0)
    both = oh1 + oh2
    bm = both.shape[0]
    earlier = (lax.broadcasted_iota(jnp.int32, (bm, bm), 0) >
               lax.broadcasted_iota(jnp.int32, (bm, bm), 1))
    prefix = _dot(jnp.where(earlier, 1.0, 0.0).astype(BF16), both.astype(BF16)) + run_sc[0:1, :]
    s1 = jnp.sum(oh1 * prefix, axis=-1, keepdims=True)
    s2 = jnp.sum(oh2 * prefix, axis=-1, keepdims=True)
    run_sc[...] = run_sc[...] + jnp.sum(both, axis=0, keepdims=True)
    cnt_ref[...] = run_sc[...]
    r_ref[...] = jnp.where(lane == 0, i1, jnp.where(lane == 1, i2,
                           jnp.where(lane == 2, g1, jnp.where(lane == 3, g2,
                                     jnp.where(lane == 4, s1, jnp.where(lane == 5, s2, 0.0))))))


def rmsnorm_router(x, g, wr, bm=512):
    n, d = x.shape
    sub = 8
    return pl.pallas_call(
        _rms_router_kernel,
        grid=(n // bm,),
        in_specs=[pl.BlockSpec((bm, d), lambda i: (i, 0)),
                  pl.BlockSpec((1, d), lambda i: (0, 0)),
                  pl.BlockSpec((d, LANES), lambda i: (0, 0))],
        out_specs=[pl.BlockSpec((bm, d), lambda i: (i, 0)),
                   pl.BlockSpec((bm, LANES), lambda i: (i, 0)),
                   pl.BlockSpec((sub, LANES), lambda i: (0, 0))],
        out_shape=[jax.ShapeDtypeStruct((n, d), F32),
                   jax.ShapeDtypeStruct((n, LANES), F32),
                   jax.ShapeDtypeStruct((sub, LANES), F32)],
        scratch_shapes=[pltpu.VMEM((sub, LANES), F32)],
        compiler_params=_params(("arbitrary",)),
        name="rmsnorm_router",
    )(x, g.reshape(1, d), wr)


def _mm_kernel(x_ref, w_ref, o_ref):
    o_ref[...] = _dot(x_ref[...], w_ref[...]).astype(o_ref.dtype)


def _mm_res_kernel(x_ref, w_ref, r_ref, o_ref):
    o_ref[...] = (r_ref[...] + _dot(x_ref[...], w_ref[...])).astype(o_ref.dtype)


def matmul(x, w, out_dtype, bm, bn, residual=None, name="matmul"):
    m, k = x.shape
    n = w.shape[1]
    in_specs = [pl.BlockSpec((bm, k), lambda i, j: (i, 0)),
                pl.BlockSpec((k, bn), lambda i, j: (0, j))]
    args = [x, w]
    body = _mm_kernel
    if residual is not None:
        in_specs.append(pl.BlockSpec((bm, bn), lambda i, j: (i, j)))
        args.append(residual)
        body = _mm_res_kernel
    return pl.pallas_call(
        body,
        grid=(m // bm, n // bn),
        in_specs=in_specs,
        out_specs=pl.BlockSpec((bm, bn), lambda i, j: (i, j)),
        out_shape=jax.ShapeDtypeStruct((m, n), out_dtype),
        compiler_params=_params(("parallel", "parallel")),
        name=name,
    )(*args)


def _conv_kernel(h_ref, wb_ref, wc_ref, wu_ref, cw_ref, o_ref):
    h = h_ref[...]
    v = _dot(h, wc_ref[...]) * _dot(h, wu_ref[...])
    row = lax.broadcasted_iota(jnp.int32, v.shape, 0)
    v1 = jnp.where(row >= 1, pltpu.roll(v, 1, 0), 0.0)
    v2 = jnp.where(row >= 2, pltpu.roll(v, 2, 0), 0.0)
    cw = cw_ref[...]
    y = cw[2:3, :] * v + cw[1:2, :] * v1 + cw[0:1, :] * v2
    o_ref[...] = (_dot(h, wb_ref[...]) * y).astype(o_ref.dtype)


def conv_branch(h, wb, wc, wu, conv_w, seq, bn=256):
    n, d = h.shape
    c = wb.shape[1]
    wspec = pl.BlockSpec((d, bn), lambda b, j: (0, j))
    return pl.pallas_call(
        _conv_kernel,
        grid=(n // seq, c // bn),
        in_specs=[pl.BlockSpec((seq, d), lambda b, j: (b, 0)), wspec, wspec, wspec,
                  pl.BlockSpec((conv_w.shape[0], bn), lambda b, j: (0, j))],
        out_specs=pl.BlockSpec((seq, bn), lambda b, j: (b, j)),
        out_shape=jax.ShapeDtypeStruct((n, c), BF16),
        compiler_params=_params(("parallel", "parallel")),
        name="conv_branch",
    )(h, wb, wc, wu, conv_w)


def _mix_kernel(yc_ref, at_ref, h_ref, wco_ref, wao_ref, wgc_ref, wga_ref, o_ref):
    h = h_ref[...]
    g_conv = jax.nn.sigmoid(_dot(h, wgc_ref[...]))
    g_attn = jax.nn.sigmoid(_dot(h, wga_ref[...]))
    y_conv = _dot(yc_ref[...], wco_ref[...])
    y_attn = _dot(at_ref[...], wao_ref[...])
    o_ref[...] = (g_conv * y_conv + g_attn * y_attn).astype(o_ref.dtype)


def gated_mix(yc, at, h, wco, wao, wgc, wga, bm=1024, bn=512):
    n, d = h.shape
    c = yc.shape[1]
    a = at.shape[1]
    return pl.pallas_call(
        _mix_kernel,
        grid=(n // bm, d // bn),
        in_specs=[pl.BlockSpec((bm, c), lambda i, j: (i, 0)),
                  pl.BlockSpec((bm, a), lambda i, j: (i, 0)),
                  pl.BlockSpec((bm, d), lambda i, j: (i, 0)),
                  pl.BlockSpec((c, bn), lambda i, j: (0, j)),
                  pl.BlockSpec((a, bn), lambda i, j: (0, j)),
                  pl.BlockSpec((d, bn), lambda i, j: (0, j)),
                  pl.BlockSpec((d, bn), lambda i, j: (0, j))],
        out_specs=pl.BlockSpec((bm, bn), lambda i, j: (i, j)),
        out_shape=jax.ShapeDtypeStruct((n, d), BF16),
        compiler_params=_params(("parallel", "parallel")),
        name="gated_mix",
    )(yc, at, h, wco, wao, wgc, wga)


def _swiglu_up_kernel(x_ref, w1_ref, w3_ref, o_ref):
    x = x_ref[...]
    o_ref[...] = (jax.nn.silu(_dot(x, w1_ref[...])) * _dot(x, w3_ref[...])).astype(o_ref.dtype)


def swiglu_up(x, w1, w3, bm=1024, bn=512):
    n, d = x.shape
    f = w1.shape[1]
    return pl.pallas_call(
        _swiglu_up_kernel,
        grid=(n // bm, f // bn),
        in_specs=[pl.BlockSpec((bm, d), lambda i, j: (i, 0)),
                  pl.BlockSpec((d, bn), lambda i, j: (0, j)),
                  pl.BlockSpec((d, bn), lambda i, j: (0, j))],
        out_specs=pl.BlockSpec((bm, bn), lambda i, j: (i, j)),
        out_shape=jax.ShapeDtypeStruct((n, f), BF16),
        compiler_params=_params(("parallel", "parallel")),
        name="swiglu_up",
    )(x, w1, w3)


def _dsa_kernel(q_ref, qi_ref, k_ref, v_ref, kbd_ref, wi_ref, o_ref,
                skey_sc, bias_sc, s_sc, m_sc, acc_sc, *, topk):
    qb = pl.program_id(1)
    n_ch = qb + 1
    q_pos = qb * Q_TILE + lax.broadcasted_iota(jnp.int32, (1, Q_TILE), 1)

    w_t = wi_ref[...] * (1.0 / 32.0)
    qi = qi_ref[...]

    def score_chunk(c, carry):
        ks = pl.multiple_of(c * K_TILE, K_TILE)
        k_pair = jnp.concatenate([kbd_ref[pl.ds(ks, K_TILE), 0:LANES],
                                  kbd_ref[pl.ds(ks, K_TILE), LANES:2 * LANES]],
                                 axis=0)
        acc = jnp.zeros((K_TILE, Q_TILE), F32)
        for p in range(IDX_HEADS // 2):
            d = jnp.maximum(_dot_nt(k_pair, qi[:, p * LANES:(p + 1) * LANES]), 0.0)
            acc = acc + w_t[2 * p:2 * p + 1, :] * d[:K_TILE, :]
            acc = acc + w_t[2 * p + 1:2 * p + 2, :] * d[K_TILE:, :]
        key_pos = ks + lax.broadcasted_iota(jnp.int32, (K_TILE, 1), 0)
        bits = pltpu.bitcast(acc, jnp.int32)
        skey = bits ^ ((bits >> 31) & 0x7FFFFFFF)
        skey_sc[c] = jnp.where(key_pos <= q_pos, skey, INT_MIN)
        return carry

    lax.fori_loop(0, n_ch, score_chunk, 0)

    sub = 8

    def count_ge(cand):
        def body(c, part):
            sk = skey_sc[c]
            ge = jnp.where(sk >= cand, 1.0, 0.0)
            for r in range(0, K_TILE, sub):
                part = part + ge[r:r + sub, :]
            return part
        part = lax.fori_loop(0, n_ch, body, jnp.zeros((sub, Q_TILE), F32))
        return jnp.sum(part, axis=0, keepdims=True)

    kf = float(topk)
    zero = jnp.zeros((1, Q_TILE), jnp.int32)
    tau0 = jnp.where(count_ge(zero) >= kf, zero, INT_MIN)

    def bit_step(i, tau):
        cand = tau | lax.shift_left(jnp.int32(1), 30 - i)
        return jnp.where(count_ge(cand) >= kf, cand, tau)

    tau = lax.fori_loop(0, 31, bit_step, tau0)
    tau = jnp.maximum(tau, INT_MIN + 1)

    def bias_chunk(c, carry):
        bias_sc[c] = jnp.where(skey_sc[c] >= tau, 0.0, NEG_BIG).T
        return carry

    lax.fori_loop(0, n_ch, bias_chunk, 0)
    lane_tiles = K_TILE // LANES

    scale = HEAD_DIM ** -0.5
    q = q_ref[...]
    rows = GQA_GROUP * Q_TILE
    for g in range(N_KV_HEADS):
        qg = jnp.concatenate(
            [q[:, (g * GQA_GROUP + j) * HEAD_DIM:(g * GQA_GROUP + j + 1) * HEAD_DIM]
             for j in range(GQA_GROUP)], axis=0)
        m_sc[...] = jnp.full((rows, LANES), NEG_BIG, F32)
        acc_sc[...] = jnp.zeros((rows, 2 * HEAD_DIM), F32)

        def logits_chunk(c, carry):
            ks = pl.multiple_of(c * K_TILE, K_TILE)
            kc = k_ref[pl.ds(ks, K_TILE), g * HEAD_DIM:(g + 1) * HEAD_DIM]
            s = _dot_nt(qg, kc) * scale + jnp.concatenate([bias_sc[c]] * GQA_GROUP, axis=0)
            s_sc[c] = s
            m = m_sc[...]
            for t in range(lane_tiles):
                m = jnp.maximum(m, s[:, t * LANES:(t + 1) * LANES])
            m_sc[...] = m
            return carry

        lax.fori_loop(0, n_ch, logits_chunk, 0)
        m_row = jnp.broadcast_to(jnp.max(m_sc[...], axis=-1, keepdims=True), (rows, LANES))

        def pv_chunk(c, carry):
            ks = pl.multiple_of(c * K_TILE, K_TILE)
            vc = v_ref[pl.ds(ks, K_TILE), g * HEAD_DIM:(g + 1) * HEAD_DIM]
            v_ones = jnp.concatenate([vc, jnp.ones_like(vc)], axis=1)
            s = s_sc[c]
            p = jnp.concatenate([jnp.exp(s[:, t * LANES:(t + 1) * LANES] - m_row)
                                 for t in range(lane_tiles)], axis=1)
            acc_sc[...] += _dot(p.astype(BF16), v_ones)
            return carry

        lax.fori_loop(0, n_ch, pv_chunk, 0)
        acc = acc_sc[...]
        out = acc[:, :HEAD_DIM] / acc[:, HEAD_DIM:]
        for j in range(GQA_GROUP):
            hh = g * GQA_GROUP + j
            o_ref[:, hh * HEAD_DIM:(hh + 1) * HEAD_DIM] = (
                out[j * Q_TILE:(j + 1) * Q_TILE, :].astype(o_ref.dtype))


def dsa_attention(att_in, w_idx, bsz, seq):
    n = att_in.shape[0]
    nq = seq // Q_TILE
    topk = min(TOPK_MAX, seq // 4)
    kv_blk = ATTN_WIDTH * 2 // KV_WIDTH
    row = lambda b, i: b * nq + i
    w_idx_t = w_idx[:, :IDX_HEADS].reshape(n // Q_TILE, Q_TILE, IDX_HEADS).transpose(0, 2, 1)
    return pl.pallas_call(
        functools.partial(_dsa_kernel, topk=topk),
        grid=(bsz, nq),
        in_specs=[pl.BlockSpec((Q_TILE, ATTN_WIDTH), lambda b, i: (row(b, i), 0)),
                  pl.BlockSpec((Q_TILE, IDX_HEADS * IDX_DIM), lambda b, i: (row(b, i), 1)),
                  pl.BlockSpec((seq, KV_WIDTH), lambda b, i: (b, kv_blk)),
                  pl.BlockSpec((seq, KV_WIDTH), lambda b, i: (b, kv_blk + 1)),
                  pl.BlockSpec((seq, 2 * LANES), lambda b, i: (b, kv_blk + 2)),
                  pl.BlockSpec((None, IDX_HEADS, Q_TILE), lambda b, i: (row(b, i), 0, 0))],
        out_specs=pl.BlockSpec((Q_TILE, ATTN_WIDTH), lambda b, i: (row(b, i), 0)),
        out_shape=jax.ShapeDtypeStruct((n, ATTN_WIDTH), BF16),
        scratch_shapes=[pltpu.VMEM((seq // K_TILE, K_TILE, Q_TILE), jnp.int32),
                        pltpu.VMEM((seq // K_TILE, Q_TILE, K_TILE), F32),
                        pltpu.VMEM((seq // K_TILE, GQA_GROUP * Q_TILE, K_TILE), F32),
                        pltpu.VMEM((GQA_GROUP * Q_TILE, LANES), F32),
                        pltpu.VMEM((GQA_GROUP * Q_TILE, 2 * HEAD_DIM), F32)],
        compiler_params=_params(("parallel", "parallel")),
        name="dsa_attention",
    )(att_in, att_in, att_in, att_in, att_in, w_idx_t)


def _row_copy(src_ref, src_row, dst_ref, dst_row, sem):
    return pltpu.make_async_copy(src_ref.at[pl.ds(src_row, 1), :], dst_ref.at[pl.ds(dst_row, 1), :], sem)


def _dispatch_kernel(dest_ref, h_ref, xs_in, xs_out, sem, *, bt):
    del xs_in

    def start(r, carry):
        for k in range(TOP_K_EXPERTS):
            _row_copy(h_ref, r, xs_out, dest_ref[r * TOP_K_EXPERTS + k], sem).start()
        return carry

    def wait(r, carry):
        for k in range(TOP_K_EXPERTS):
            _row_copy(h_ref, r, xs_out, dest_ref[r * TOP_K_EXPERTS + k], sem).wait()
        return carry

    lax.fori_loop(0, bt, start, 0)
    lax.fori_loop(0, bt, wait, 0)


def moe_dispatch(h, dest, n_rows, bt=512):
    n, d = h.shape
    xs0 = jnp.zeros((n_rows, d), h.dtype)
    return pl.pallas_call(
        functools.partial(_dispatch_kernel, bt=bt),
        grid=(n // bt,),
        in_specs=[pl.BlockSpec((bt * TOP_K_EXPERTS,), lambda i: (i,), memory_space=pltpu.SMEM),
                  pl.BlockSpec((bt, d), lambda i: (i, 0)),
                  pl.BlockSpec(memory_space=pl.ANY)],
        out_specs=pl.BlockSpec(memory_space=pl.ANY),
        out_shape=jax.ShapeDtypeStruct((n_rows, d), h.dtype),
        scratch_shapes=[pltpu.SemaphoreType.DMA(())],
        input_output_aliases={2: 0},
        compiler_params=pltpu.CompilerParams(dimension_semantics=("arbitrary",),
                                             has_side_effects=True),
        name="moe_dispatch",
    )(dest.reshape(-1), h, xs0)


def _combine_kernel(dest_ref, x_ref, r_ref, y_hbm, o_ref, buf, sem, *, bt):
    def start(r, carry):
        for k in range(TOP_K_EXPERTS):
            _row_copy(y_hbm, dest_ref[r * TOP_K_EXPERTS + k], buf.at[k], r, sem).start()
        return carry

    def wait(r, carry):
        for k in range(TOP_K_EXPERTS):
            _row_copy(y_hbm, dest_ref[r * TOP_K_EXPERTS + k], buf.at[k], r, sem).wait()
        return carry

    lax.fori_loop(0, bt, start, 0)
    lax.fori_loop(0, bt, wait, 0)
    r = r_ref[...]
    o_ref[...] = x_ref[...] + (buf[0] * r[:, 2:3] + buf[1] * r[:, 3:4])


def moe_combine(x, rinfo, y, dest, bt=512):
    n, d = x.shape
    return pl.pallas_call(
        functools.partial(_combine_kernel, bt=bt),
        grid=(n // bt,),
        in_specs=[pl.BlockSpec((bt * TOP_K_EXPERTS,), lambda i: (i,), memory_space=pltpu.SMEM),
                  pl.BlockSpec((bt, d), lambda i: (i, 0)),
                  pl.BlockSpec((bt, LANES), lambda i: (i, 0)),
                  pl.BlockSpec(memory_space=pl.ANY)],
        out_specs=pl.BlockSpec((bt, d), lambda i: (i, 0)),
        out_shape=jax.ShapeDtypeStruct((n, d), x.dtype),
        scratch_shapes=[pltpu.VMEM((TOP_K_EXPERTS, bt, d), y.dtype), pltpu.SemaphoreType.DMA(())],
        compiler_params=_params(("arbitrary",)),
        name="moe_combine",
    )(dest.reshape(-1), x, rinfo, y)


def _moe_up_kernel(be_ref, nu_ref, x_ref, w1_ref, w3_ref, o_ref):
    i = pl.program_id(1)

    @pl.when(i < nu_ref[0])
    def _():
        x = x_ref[...].astype(BF16)
        o_ref[...] = (jax.nn.silu(_dot(x, w1_ref[...])) * _dot(x, w3_ref[...])).astype(o_ref.dtype)

    @pl.when(i >= nu_ref[0])
    def _():
        o_ref[...] = jnp.zeros_like(o_ref)


def _moe_down_kernel(be_ref, nu_ref, a_ref, w2_ref, o_ref):
    i = pl.program_id(1)

    @pl.when(i < nu_ref[0])
    def _():
        o_ref[...] = _dot(a_ref[...], w2_ref[...])

    @pl.when(i >= nu_ref[0])
    def _():
        o_ref[...] = jnp.zeros_like(o_ref)


def moe_experts(xs, block_e, n_used, w1, w3, w2, bn_up=1024, bn_down=512):
    n_rows, d = xs.shape
    nb = n_rows // MOE_TILE
    f = w1.shape[2]
    blk = lambda i, nu: jnp.minimum(i, nu[0] - 1)
    a = pl.pallas_call(
        _moe_up_kernel,
        grid_spec=pltpu.PrefetchScalarGridSpec(
            num_scalar_prefetch=2,
            grid=(f // bn_up, nb),
            in_specs=[pl.BlockSpec((MOE_TILE, d), lambda j, i, be, nu: (blk(i, nu), 0)),
                      pl.BlockSpec((None, d, bn_up), lambda j, i, be, nu: (be[i], 0, j)),
                      pl.BlockSpec((None, d, bn_up), lambda j, i, be, nu: (be[i], 0, j))],
            out_specs=pl.BlockSpec((MOE_TILE, bn_up), lambda j, i, be, nu: (i, j))),
        out_shape=jax.ShapeDtypeStruct((n_rows, f), BF16),
        compiler_params=_params(("arbitrary", "arbitrary")),
        name="moe_up",
    )(block_e, n_used, xs, w1, w3)
    return pl.pallas_call(
        _moe_down_kernel,
        grid_spec=pltpu.PrefetchScalarGridSpec(
            num_scalar_prefetch=2,
            grid=(d // bn_down, nb),
            in_specs=[pl.BlockSpec((MOE_TILE, f), lambda j, i, be, nu: (blk(i, nu), 0)),
                      pl.BlockSpec((None, f, bn_down), lambda j, i, be, nu: (be[i], 0, j))],
            out_specs=pl.BlockSpec((MOE_TILE, bn_down), lambda j, i, be, nu: (i, j))),
        out_shape=jax.ShapeDtypeStruct((n_rows, d), F32),
        compiler_params=_params(("arbitrary", "arbitrary")),
        name="moe_down",
    )(block_e, n_used, a, w2)


def moe_layer(x, h, rinfo, counts_f, w1, w3, w2):
    n_tok, d = x.shape
    n_assign = n_tok * TOP_K_EXPERTS
    top_idx = rinfo[:, 0:TOP_K_EXPERTS].astype(jnp.int32)
    slot = rinfo[:, 2 * TOP_K_EXPERTS:3 * TOP_K_EXPERTS].astype(jnp.int32)
    counts = counts_f[0, :N_EXPERTS].astype(jnp.int32)
    padded = ((counts + MOE_TILE - 1) // MOE_TILE) * MOE_TILE
    ends_padded = jnp.cumsum(padded)
    start_padded = ends_padded - padded
    dest = start_padded[top_idx] + slot
    nb = -(-(n_assign + N_EXPERTS * (MOE_TILE - 1)) // MOE_TILE)
    n_used = (ends_padded[-1] // MOE_TILE).astype(jnp.int32).reshape(1)
    block_e = jnp.searchsorted(ends_padded, jnp.arange(nb, dtype=jnp.int32) * MOE_TILE, side='right')
    last_e = jnp.max(jnp.where(counts > 0, jnp.arange(N_EXPERTS), 0))
    block_e = jnp.minimum(block_e, last_e).astype(jnp.int32)
    xs = moe_dispatch(h, dest, nb * MOE_TILE)
    y = moe_experts(xs, block_e, n_used, w1, w3, w2)
    return moe_combine(x, rinfo, y, dest)


def _in_proj_weights(w):
    o = 0
    parts = []
    for width in (CONV_CH, CONV_CH, CONV_CH, ATTN_WIDTH, KV_WIDTH, KV_WIDTH,
                  IDX_HEADS * IDX_DIM, IDX_DIM, IDX_HEADS, 2 * D_MODEL):
        parts.append(w[:, o:o + width])
        o += width
    wb, wc, wu, wq, wk, wv, wqi, wki, wwi, wg = parts
    z = jnp.zeros_like(wki)
    w_att = jnp.concatenate([wq, wqi, wk, wv, wki, z, z, wki], axis=1).astype(BF16)
    w_wi = jnp.pad(wwi, ((0, 0), (0, LANES - IDX_HEADS))).astype(BF16)
    return (wb.astype(BF16), wc.astype(BF16), wu.astype(BF16), w_att, w_wi,
            wg[:, :D_MODEL].astype(BF16), wg[:, D_MODEL:].astype(BF16))


@jax.jit
def _forward(x, mix_norm, w_in, conv_w, w_conv_out, w_attn_out, w_out, ffn_norm,
             ffn_w1, ffn_w3, ffn_w2, router_w, exp_w1, exp_w3, exp_w2, final_norm):
    bsz, seq, d = x.shape
    n = bsz * seq
    depth = w_in.shape[0]
    xf = x.reshape(n, d)
    for layer in range(depth):
        wb, wc, wu, w_att, w_wi, wgc, wga = _in_proj_weights(w_in[layer])
        h = rmsnorm(xf, mix_norm[layer], BF16)
        yc = conv_branch(h, wb, wc, wu, conv_w[layer], seq)
        att_in = matmul(h, w_att, BF16, 1024, w_att.shape[1] // 2, name="attn_in_proj")
        w_idx = matmul(h, w_wi, F32, 1024, LANES, name="idx_weight_proj")
        at = dsa_attention(att_in, w_idx, bsz, seq)
        mix = gated_mix(yc, at, h, w_conv_out[layer].astype(BF16), w_attn_out[layer].astype(BF16),
                        wgc, wga)
        xf = matmul(mix, w_out[layer].astype(BF16), F32, 1024, 1024, residual=xf, name="out_proj")
        j = layer // 2
        if layer % 2 == 0:
            h2 = rmsnorm(xf, ffn_norm[layer], BF16)
            a = swiglu_up(h2, ffn_w1[j].astype(BF16), ffn_w3[j].astype(BF16))
            xf = matmul(a, ffn_w2[j].astype(BF16), F32, 1024, 512, residual=xf, name="ffn_down")
        else:
            wr = jnp.pad(router_w[j], ((0, 0), (0, LANES - N_EXPERTS))).astype(BF16)
            h2, rinfo, counts = rmsnorm_router(xf, ffn_norm[layer], wr)
            xf = moe_layer(xf, h2, rinfo, counts, exp_w1[j].astype(BF16), exp_w3[j].astype(BF16),
                           exp_w2[j].astype(BF16))
    return rmsnorm(xf, final_norm, F32).reshape(bsz, seq, d)


def kernel(x, mix_norm, w_in, conv_w, w_conv_out, w_attn_out, w_out, ffn_norm, ffn_w1, ffn_w3,
           ffn_w2, router_w, exp_w1, exp_w3, exp_w2, final_norm):
    return _forward(x, mix_norm, w_in, conv_w, w_conv_out, w_attn_out, w_out, ffn_norm,
                    ffn_w1, ffn_w3, ffn_w2, router_w, exp_w1, exp_w3, exp_w2, final_norm)
```

```python
import functools

import jax
import jax.numpy as jnp
from jax import lax
from jax.experimental import pallas as pl
from jax.experimental.pallas import tpu as pltpu

BF16 = jnp.bfloat16
F32 = jnp.float32

D_MODEL = 2048
CONV_CH = 1024
N_HEADS = 8
HEAD_DIM = 128
N_KV_HEADS = 2
GQA_GROUP = N_HEADS // N_KV_HEADS
ATTN_WIDTH = N_HEADS * HEAD_DIM
KV_WIDTH = N_KV_HEADS * HEAD_DIM
IDX_HEADS = 16
IDX_DIM = 64
TOPK_MAX = 256
N_EXPERTS = 8
TOP_K_EXPERTS = 2
RMS_EPS = 1e-6

LANES = 128
VMEM_LIMIT = 58 * 1024 * 1024
INT_MIN = -(2 ** 31)
NEG_BIG = -1e30

Q_TILE = 256
K_TILE = 256
COUNT_ROWS = 128
MOE_TILE = 512


def _params(sem):
    return pltpu.CompilerParams(dimension_semantics=sem, vmem_limit_bytes=VMEM_LIMIT)


def _dot(a, b):
    return jnp.dot(a, b, preferred_element_type=F32)


def _dot_nt(a, b):
    return lax.dot_general(a, b, (((1,), (1,)), ((), ())), preferred_element_type=F32)


def _rms(x, g):
    ms = jnp.mean(x * x, axis=-1, keepdims=True)
    return x * lax.rsqrt(ms + RMS_EPS) * g


def _rms_kernel(x_ref, g_ref, o_ref):
    o_ref[...] = _rms(x_ref[...], g_ref[...]).astype(o_ref.dtype)


def rmsnorm(x, g, out_dtype, bm=512):
    n, d = x.shape
    return pl.pallas_call(
        _rms_kernel,
        grid=(n // bm,),
        in_specs=[pl.BlockSpec((bm, d), lambda i: (i, 0)),
                  pl.BlockSpec((1, d), lambda i: (0, 0))],
        out_specs=pl.BlockSpec((bm, d), lambda i: (i, 0)),
        out_shape=jax.ShapeDtypeStruct((n, d), out_dtype),
        compiler_params=_params(("parallel",)),
        name="rmsnorm",
    )(x, g.reshape(1, d))


def _rms_router_kernel(x_ref, g_ref, wr_ref, h_ref, r_ref, cnt_ref, run_sc):
    @pl.when(pl.program_id(0) == 0)
    def _():
        run_sc[...] = jnp.zeros_like(run_sc)

    hf = _rms(x_ref[...], g_ref[...])
    h_ref[...] = hf
    logits = _dot(hf.astype(BF16), wr_ref[...])
    lane = lax.broadcasted_iota(jnp.int32, logits.shape, 1)
    lane_f = lane.astype(F32)
    lg = jnp.where(lane < N_EXPERTS, logits, -jnp.inf)
    m1 = jnp.max(lg, axis=-1, keepdims=True)
    i1 = jnp.min(jnp.where(lg == m1, lane_f, float(LANES)), axis=-1, keepdims=True)
    lg2 = jnp.where(lane_f == i1, -jnp.inf, lg)
    m2 = jnp.max(lg2, axis=-1, keepdims=True)
    i2 = jnp.min(jnp.where(lg2 == m2, lane_f, float(LANES)), axis=-1, keepdims=True)
    e2 = jnp.exp(m2 - m1)
    den = 1.0 + e2
    g1 = 1.0 / den
    g2 = e2 / den
    oh1 = jnp.where(lane_f == i1, 1.0, 0.0)
    oh2 = jnp.where(lane_f == i2, 1.0, 0.0)
    both = oh1 + oh2
    bm = both.shape[0]
    earlier = (lax.broadcasted_iota(jnp.int32, (bm, bm), 0) >
               lax.broadcasted_iota(jnp.int32, (bm, bm), 1))
    prefix = _dot(jnp.where(earlier, 1.0, 0.0).astype(BF16), both.astype(BF16)) + run_sc[0:1, :]
    s1 = jnp.sum(oh1 * prefix, axis=-1, keepdims=True)
    s2 = jnp.sum(oh2 * prefix, axis=-1, keepdims=True)
    run_sc[...] = run_sc[...] + jnp.sum(both, axis=0, keepdims=True)
    cnt_ref[...] = run_sc[...]
    r_ref[...] = jnp.where(lane == 0, i1, jnp.where(lane == 1, i2,
                           jnp.where(lane == 2, g1, jnp.where(lane == 3, g2,
                                     jnp.where(lane == 4, s1, jnp.where(lane == 5, s2, 0.0))))))


def rmsnorm_router(x, g, wr, bm=512):
    n, d = x.shape
    sub = 8
    return pl.pallas_call(
        _rms_router_kernel,
        grid=(n // bm,),
        in_specs=[pl.BlockSpec((bm, d), lambda i: (i, 0)),
                  pl.BlockSpec((1, d), lambda i: (0, 0)),
                  pl.BlockSpec((d, LANES), lambda i: (0, 0))],
        out_specs=[pl.BlockSpec((bm, d), lambda i: (i, 0)),
                   pl.BlockSpec((bm, LANES), lambda i: (i, 0)),
                   pl.BlockSpec((sub, LANES), lambda i: (0, 0))],
        out_shape=[jax.ShapeDtypeStruct((n, d), F32),
                   jax.ShapeDtypeStruct((n, LANES), F32),
                   jax.ShapeDtypeStruct((sub, LANES), F32)],
        scratch_shapes=[pltpu.VMEM((sub, LANES), F32)],
        compiler_params=_params(("arbitrary",)),
        name="rmsnorm_router",
    )(x, g.reshape(1, d), wr)


def _mm_kernel(x_ref, w_ref, o_ref):
    o_ref[...] = _dot(x_ref[...], w_ref[...]).astype(o_ref.dtype)


def _mm_res_kernel(x_ref, w_ref, r_ref, o_ref):
    o_ref[...] = (r_ref[...] + _dot(x_ref[...], w_ref[...])).astype(o_ref.dtype)


def matmul(x, w, out_dtype, bm, bn, residual=None, name="matmul"):
    m, k = x.shape
    n = w.shape[1]
    in_specs = [pl.BlockSpec((bm, k), lambda i, j: (i, 0)),
                pl.BlockSpec((k, bn), lambda i, j: (0, j))]
    args = [x, w]
    body = _mm_kernel
    if residual is not None:
        in_specs.append(pl.BlockSpec((bm, bn), lambda i, j: (i, j)))
        args.append(residual)
        body = _mm_res_kernel
    return pl.pallas_call(
        body,
        grid=(m // bm, n // bn),
        in_specs=in_specs,
        out_specs=pl.BlockSpec((bm, bn), lambda i, j: (i, j)),
        out_shape=jax.ShapeDtypeStruct((m, n), out_dtype),
        compiler_params=_params(("parallel", "parallel")),
        name=name,
    )(*args)


def _conv_kernel(h_ref, wb_ref, wc_ref, wu_ref, cw_ref, o_ref):
    h = h_ref[...]
    v = _dot(h, wc_ref[...]) * _dot(h, wu_ref[...])
    row = lax.broadcasted_iota(jnp.int32, v.shape, 0)
    v1 = jnp.where(row >= 1, pltpu.roll(v, 1, 0), 0.0)
    v2 = jnp.where(row >= 2, pltpu.roll(v, 2, 0), 0.0)
    cw = cw_ref[...]
    y = cw[2:3, :] * v + cw[1:2, :] * v1 + cw[0:1, :] * v2
    o_ref[...] = (_dot(h, wb_ref[...]) * y).astype(o_ref.dtype)


def conv_branch(h, wb, wc, wu, conv_w, seq, bn=256):
    n, d = h.shape
    c = wb.shape[1]
    wspec = pl.BlockSpec((d, bn), lambda b, j: (0, j))
    return pl.pallas_call(
        _conv_kernel,
        grid=(n // seq, c // bn),
        in_specs=[pl.BlockSpec((seq, d), lambda b, j: (b, 0)), wspec, wspec, wspec,
                  pl.BlockSpec((conv_w.shape[0], bn), lambda b, j: (0, j))],
        out_specs=pl.BlockSpec((seq, bn), lambda b, j: (b, j)),
        out_shape=jax.ShapeDtypeStruct((n, c), BF16),
        compiler_params=_params(("parallel", "parallel")),
        name="conv_branch",
    )(h, wb, wc, wu, conv_w)


def _mix_kernel(yc_ref, at_ref, h_ref, wco_ref, wao_ref, wgc_ref, wga_ref, o_ref):
    h = h_ref[...]
    g_conv = jax.nn.sigmoid(_dot(h, wgc_ref[...]))
    g_attn = jax.nn.sigmoid(_dot(h, wga_ref[...]))
    y_conv = _dot(yc_ref[...], wco_ref[...])
    y_attn = _dot(at_ref[...], wao_ref[...])
    o_ref[...] = (g_conv * y_conv + g_attn * y_attn).astype(o_ref.dtype)


def gated_mix(yc, at, h, wco, wao, wgc, wga, bm=1024, bn=512):
    n, d = h.shape
    c = yc.shape[1]
    a = at.shape[1]
    return pl.pallas_call(
        _mix_kernel,
        grid=(n // bm, d // bn),
        in_specs=[pl.BlockSpec((bm, c), lambda i, j: (i, 0)),
                  pl.BlockSpec((bm, a), lambda i, j: (i, 0)),
                  pl.BlockSpec((bm, d), lambda i, j: (i, 0)),
                  pl.BlockSpec((c, bn), lambda i, j: (0, j)),
                  pl.BlockSpec((a, bn), lambda i, j: (0, j)),
                  pl.BlockSpec((d, bn), lambda i, j: (0, j)),
                  pl.BlockSpec((d, bn), lambda i, j: (0, j))],
        out_specs=pl.BlockSpec((bm, bn), lambda i, j: (i, j)),
        out_shape=jax.ShapeDtypeStruct((n, d), BF16),
        compiler_params=_params(("parallel", "parallel")),
        name="gated_mix",
    )(yc, at, h, wco, wao, wgc, wga)


def _swiglu_up_kernel(x_ref, w1_ref, w3_ref, o_ref):
    x = x_ref[...]
    o_ref[...] = (jax.nn.silu(_dot(x, w1_ref[...])) * _dot(x, w3_ref[...])).astype(o_ref.dtype)


def swiglu_up(x, w1, w3, bm=1024, bn=512):
    n, d = x.shape
    f = w1.shape[1]
    return pl.pallas_call(
        _swiglu_up_kernel,
        grid=(n // bm, f // bn),
        in_specs=[pl.BlockSpec((bm, d), lambda i, j: (i, 0)),
                  pl.BlockSpec((d, bn), lambda i, j: (0, j)),
                  pl.BlockSpec((d, bn), lambda i, j: (0, j))],
        out_specs=pl.BlockSpec((bm, bn), lambda i, j: (i, j)),
        out_shape=jax.ShapeDtypeStruct((n, f), BF16),
        compiler_params=_params(("parallel", "parallel")),
        name="swiglu_up",
    )(x, w1, w3)


def _dsa_kernel(q_ref, qi_ref, k_ref, v_ref, kbd_ref, wi_ref, o_ref,
                skey_sc, bias_sc, s_sc, m_sc, acc_sc, *, topk):
    qb = pl.program_id(1)
    n_ch = qb + 1
    q_pos = qb * Q_TILE + lax.broadcasted_iota(jnp.int32, (1, Q_TILE), 1)

    w_t = wi_ref[...] * (1.0 / 32.0)
    qi = qi_ref[...]

    def score_chunk(c, carry):
        ks = pl.multiple_of(c * K_TILE, K_TILE)
        k_pair = jnp.concatenate([kbd_ref[pl.ds(ks, K_TILE), 0:LANES],
                                  kbd_ref[pl.ds(ks, K_TILE), LANES:2 * LANES]],
                                 axis=0)
        acc = jnp.zeros((K_TILE, Q_TILE), F32)
        for p in range(IDX_HEADS // 2):
            d = jnp.maximum(_dot_nt(k_pair, qi[:, p * LANES:(p + 1) * LANES]), 0.0)
            acc = acc + w_t[2 * p:2 * p + 1, :] * d[:K_TILE, :]
            acc = acc + w_t[2 * p + 1:2 * p + 2, :] * d[K_TILE:, :]
        key_pos = ks + lax.broadcasted_iota(jnp.int32, (K_TILE, 1), 0)
        bits = pltpu.bitcast(acc, jnp.int32)
        skey = bits ^ ((bits >> 31) & 0x7FFFFFFF)
        skey_sc[c] = jnp.where(key_pos <= q_pos, skey, INT_MIN)
        return carry

    lax.fori_loop(0, n_ch, score_chunk, 0)

    sub = 8

    def count_ge(cand):
        def body(c, part):
            sk = skey_sc[c]
            ge = jnp.where(sk >= cand, 1.0, 0.0)
            for r in range(0, K_TILE, sub):
                part = part + ge[r:r + sub, :]
            return part
        part = lax.fori_loop(0, n_ch, body, jnp.zeros((sub, Q_TILE), F32))
        return jnp.sum(part, axis=0, keepdims=True)

    kf = float(topk)
    zero = jnp.zeros((1, Q_TILE), jnp.int32)
    tau0 = jnp.where(count_ge(zero) >= kf, zero, INT_MIN)

    def bit_step(i, tau):
        cand = tau | lax.shift_left(jnp.int32(1), 30 - i)
        return jnp.where(count_ge(cand) >= kf, cand, tau)

    tau = lax.fori_loop(0, 31, bit_step, tau0)
    tau = jnp.maximum(tau, INT_MIN + 1)

    def bias_chunk(c, carry):
        bias_sc[c] = jnp.where(skey_sc[c] >= tau, 0.0, NEG_BIG).T
        return carry

    lax.fori_loop(0, n_ch, bias_chunk, 0)
    lane_tiles = K_TILE // LANES

    scale = HEAD_DIM ** -0.5
    q = q_ref[...]
    rows = GQA_GROUP * Q_TILE
    for g in range(N_KV_HEADS):
        qg = jnp.concatenate(
            [q[:, (g * GQA_GROUP + j) * HEAD_DIM:(g * GQA_GROUP + j + 1) * HEAD_DIM]
             for j in range(GQA_GROUP)], axis=0)
        m_sc[...] = jnp.full((rows, LANES), NEG_BIG, F32)
        acc_sc[...] = jnp.zeros((rows, 2 * HEAD_DIM), F32)

        def logits_chunk(c, carry):
            ks = pl.multiple_of(c * K_TILE, K_TILE)
            kc = k_ref[pl.ds(ks, K_TILE), g * HEAD_DIM:(g + 1) * HEAD_DIM]
            s = _dot_nt(qg, kc) * scale + jnp.concatenate([bias_sc[c]] * GQA_GROUP, axis=0)
            s_sc[c] = s
            m = m_sc[...]
            for t in range(lane_tiles):
                m = jnp.maximum(m, s[:, t * LANES:(t + 1) * LANES])
            m_sc[...] = m
            return carry

        lax.fori_loop(0, n_ch, logits_chunk, 0)
        m_row = jnp.broadcast_to(jnp.max(m_sc[...], axis=-1, keepdims=True), (rows, LANES))

        def pv_chunk(c, carry):
            ks = pl.multiple_of(c * K_TILE, K_TILE)
            vc = v_ref[pl.ds(ks, K_TILE), g * HEAD_DIM:(g + 1) * HEAD_DIM]
            v_ones = jnp.concatenate([vc, jnp.ones_like(vc)], axis=1)
            s = s_sc[c]
            p = jnp.concatenate([jnp.exp(s[:, t * LANES:(t + 1) * LANES] - m_row)
                                 for t in range(lane_tiles)], axis=1)
            acc_sc[...] += _dot(p.astype(BF16), v_ones)
            return carry

        lax.fori_loop(0, n_ch, pv_chunk, 0)
        acc = acc_sc[...]
        out = acc[:, :HEAD_DIM] / acc[:, HEAD_DIM:]
        for j in range(GQA_GROUP):
            hh = g * GQA_GROUP + j
            o_ref[:, hh * HEAD_DIM:(hh + 1) * HEAD_DIM] = (
                out[j * Q_TILE:(j + 1) * Q_TILE, :].astype(o_ref.dtype))


def dsa_attention(att_in, w_idx, bsz, seq):
    n = att_in.shape[0]
    nq = seq // Q_TILE
    topk = min(TOPK_MAX, seq // 4)
    kv_blk = ATTN_WIDTH * 2 // KV_WIDTH
    row = lambda b, i: b * nq + i
    w_idx_t = w_idx[:, :IDX_HEADS].reshape(n // Q_TILE, Q_TILE, IDX_HEADS).transpose(0, 2, 1)
    return pl.pallas_call(
        functools.partial(_dsa_kernel, topk=topk),
        grid=(bsz, nq),
        in_specs=[pl.BlockSpec((Q_TILE, ATTN_WIDTH), lambda b, i: (row(b, i), 0)),
                  pl.BlockSpec((Q_TILE, IDX_HEADS * IDX_DIM), lambda b, i: (row(b, i), 1)),
                  pl.BlockSpec((seq, KV_WIDTH), lambda b, i: (b, kv_blk)),
                  pl.BlockSpec((seq, KV_WIDTH), lambda b, i: (b, kv_blk + 1)),
                  pl.BlockSpec((seq, 2 * LANES), lambda b, i: (b, kv_blk + 2)),
                  pl.BlockSpec((None, IDX_HEADS, Q_TILE), lambda b, i: (row(b, i), 0, 0))],
        out_specs=pl.BlockSpec((Q_TILE, ATTN_WIDTH), lambda b, i: (row(b, i), 0)),
        out_shape=jax.ShapeDtypeStruct((n, ATTN_WIDTH), BF16),
        scratch_shapes=[pltpu.VMEM((seq // K_TILE, K_TILE, Q_TILE), jnp.int32),
                        pltpu.VMEM((seq // K_TILE, Q_TILE, K_TILE), F32),
                        pltpu.VMEM((seq // K_TILE, GQA_GROUP * Q_TILE, K_TILE), F32),
                        pltpu.VMEM((GQA_GROUP * Q_TILE, LANES), F32),
                        pltpu.VMEM((GQA_GROUP * Q_TILE, 2 * HEAD_DIM), F32)],
        compiler_params=_params(("parallel", "parallel")),
        name="dsa_attention",
    )(att_in, att_in, att_in, att_in, att_in, w_idx_t)


def _row_copy(src_ref, src_row, dst_ref, dst_row, sem):
    return pltpu.make_async_copy(src_ref.at[pl.ds(src_row, 1), :], dst_ref.at[pl.ds(dst_row, 1), :], sem)


def _dispatch_kernel(dest_ref, h_ref, xs_in, xs_out, sem, *, bt):
    del xs_in

    def start(r, carry):
        for k in range(TOP_K_EXPERTS):
            _row_copy(h_ref, r, xs_out, dest_ref[r * TOP_K_EXPERTS + k], sem).start()
        return carry

    def wait(r, carry):
        for k in range(TOP_K_EXPERTS):
            _row_copy(h_ref, r, xs_out, dest_ref[r * TOP_K_EXPERTS + k], sem).wait()
        return carry

    lax.fori_loop(0, bt, start, 0)
    lax.fori_loop(0, bt, wait, 0)


def moe_dispatch(h, dest, n_rows, bt=512):
    n, d = h.shape
    xs0 = jnp.zeros((n_rows, d), h.dtype)
    return pl.pallas_call(
        functools.partial(_dispatch_kernel, bt=bt),
        grid=(n // bt,),
        in_specs=[pl.BlockSpec((bt * TOP_K_EXPERTS,), lambda i: (i,), memory_space=pltpu.SMEM),
                  pl.BlockSpec((bt, d), lambda i: (i, 0)),
                  pl.BlockSpec(memory_space=pl.ANY)],
        out_specs=pl.BlockSpec(memory_space=pl.ANY),
        out_shape=jax.ShapeDtypeStruct((n_rows, d), h.dtype),
        scratch_shapes=[pltpu.SemaphoreType.DMA(())],
        input_output_aliases={2: 0},
        compiler_params=pltpu.CompilerParams(dimension_semantics=("arbitrary",),
                                             has_side_effects=True),
        name="moe_dispatch",
    )(dest.reshape(-1), h, xs0)


def _combine_kernel(dest_ref, x_ref, r_ref, y_hbm, *rest, bt, normed):
    if normed:
        gain_ref, o_ref, buf, sem = rest
    else:
        o_ref, buf, sem = rest

    def start(r, carry):
        for k in range(TOP_K_EXPERTS):
            _row_copy(y_hbm, dest_ref[r * TOP_K_EXPERTS + k], buf.at[k], r, sem).start()
        return carry

    def wait(r, carry):
        for k in range(TOP_K_EXPERTS):
            _row_copy(y_hbm, dest_ref[r * TOP_K_EXPERTS + k], buf.at[k], r, sem).wait()
        return carry

    lax.fori_loop(0, bt, start, 0)
    lax.fori_loop(0, bt, wait, 0)
    r = r_ref[...]
    out = x_ref[...] + (buf[0] * r[:, 2:3] + buf[1] * r[:, 3:4])
    if normed:
        out = _rms(out, gain_ref[...])
    o_ref[...] = out


def moe_combine(x, rinfo, y, dest, norm_gain=None, bt=512):
    n, d = x.shape
    normed = norm_gain is not None
    in_specs = [pl.BlockSpec((bt * TOP_K_EXPERTS,), lambda i: (i,), memory_space=pltpu.SMEM),
                pl.BlockSpec((bt, d), lambda i: (i, 0)),
                pl.BlockSpec((bt, LANES), lambda i: (i, 0)),
                pl.BlockSpec(memory_space=pl.ANY)]
    args = [dest.reshape(-1), x, rinfo, y]
    if normed:
        in_specs.append(pl.BlockSpec((1, d), lambda i: (0, 0)))
        args.append(norm_gain.reshape(1, d))
    return pl.pallas_call(
        functools.partial(_combine_kernel, bt=bt, normed=normed),
        grid=(n // bt,),
        in_specs=in_specs,
        out_specs=pl.BlockSpec((bt, d), lambda i: (i, 0)),
        out_shape=jax.ShapeDtypeStruct((n, d), x.dtype),
        scratch_shapes=[pltpu.VMEM((TOP_K_EXPERTS, bt, d), y.dtype), pltpu.SemaphoreType.DMA(())],
        compiler_params=_params(("arbitrary",)),
        name="moe_combine",
    )(*args)


def _moe_up_kernel(be_ref, nu_ref, x_ref, w1_ref, w3_ref, o_ref):
    i = pl.program_id(1)

    @pl.when(i < nu_ref[0])
    def _():
        x = x_ref[...].astype(BF16)
        o_ref[...] = (jax.nn.silu(_dot(x, w1_ref[...])) * _dot(x, w3_ref[...])).astype(o_ref.dtype)

    @pl.when(i >= nu_ref[0])
    def _():
        o_ref[...] = jnp.zeros_like(o_ref)


def _moe_down_kernel(be_ref, nu_ref, a_ref, w2_ref, o_ref):
    i = pl.program_id(1)

    @pl.when(i < nu_ref[0])
    def _():
        o_ref[...] = _dot(a_ref[...], w2_ref[...])

    @pl.when(i >= nu_ref[0])
    def _():
        o_ref[...] = jnp.zeros_like(o_ref)


def moe_experts(xs, block_e, n_used, w1, w3, w2, bn_up=1024, bn_down=512):
    n_rows, d = xs.shape
    nb = n_rows // MOE_TILE
    f = w1.shape[2]
    blk = lambda i, nu: jnp.minimum(i, nu[0] - 1)
    a = pl.pallas_call(
        _moe_up_kernel,
        grid_spec=pltpu.PrefetchScalarGridSpec(
            num_scalar_prefetch=2,
            grid=(f // bn_up, nb),
            in_specs=[pl.BlockSpec((MOE_TILE, d), lambda j, i, be, nu: (blk(i, nu), 0)),
                      pl.BlockSpec((None, d, bn_up), lambda j, i, be, nu: (be[i], 0, j)),
                      pl.BlockSpec((None, d, bn_up), lambda j, i, be, nu: (be[i], 0, j))],
            out_specs=pl.BlockSpec((MOE_TILE, bn_up), lambda j, i, be, nu: (i, j))),
        out_shape=jax.ShapeDtypeStruct((n_rows, f), BF16),
        compiler_params=_params(("arbitrary", "arbitrary")),
        name="moe_up",
    )(block_e, n_used, xs, w1, w3)
    return pl.pallas_call(
        _moe_down_kernel,
        grid_spec=pltpu.PrefetchScalarGridSpec(
            num_scalar_prefetch=2,
            grid=(d // bn_down, nb),
            in_specs=[pl.BlockSpec((MOE_TILE, f), lambda j, i, be, nu: (blk(i, nu), 0)),
                      pl.BlockSpec((None, f, bn_down), lambda j, i, be, nu: (be[i], 0, j))],
            out_specs=pl.BlockSpec((MOE_TILE, bn_down), lambda j, i, be, nu: (i, j))),
        out_shape=jax.ShapeDtypeStruct((n_rows, d), F32),
        compiler_params=_params(("arbitrary", "arbitrary")),
        name="moe_down",
    )(block_e, n_used, a, w2)


def moe_layer(x, h, rinfo, counts_f, w1, w3, w2, norm_gain=None):
    n_tok, d = x.shape
    n_assign = n_tok * TOP_K_EXPERTS
    top_idx = rinfo[:, 0:TOP_K_EXPERTS].astype(jnp.int32)
    slot = rinfo[:, 2 * TOP_K_EXPERTS:3 * TOP_K_EXPERTS].astype(jnp.int32)
    counts = counts_f[0, :N_EXPERTS].astype(jnp.int32)
    padded = ((counts + MOE_TILE - 1) // MOE_TILE) * MOE_TILE
    ends_padded = jnp.cumsum(padded)
    start_padded = ends_padded - padded
    dest = start_padded[top_idx] + slot
    nb = -(-(n_assign + N_EXPERTS * (MOE_TILE - 1)) // MOE_TILE)
    n_used = (ends_padded[-1] // MOE_TILE).astype(jnp.int32).reshape(1)
    block_e = jnp.searchsorted(ends_padded, jnp.arange(nb, dtype=jnp.int32) * MOE_TILE, side='right')
    last_e = jnp.max(jnp.where(counts > 0, jnp.arange(N_EXPERTS), 0))
    block_e = jnp.minimum(block_e, last_e).astype(jnp.int32)
    xs = moe_dispatch(h, dest, nb * MOE_TILE)
    y = moe_experts(xs, block_e, n_used, w1, w3, w2)
    return moe_combine(x, rinfo, y, dest, norm_gain)


def _in_proj_weights(w):
    o = 0
    parts = []
    for width in (CONV_CH, CONV_CH, CONV_CH, ATTN_WIDTH, KV_WIDTH, KV_WIDTH,
                  IDX_HEADS * IDX_DIM, IDX_DIM, IDX_HEADS, 2 * D_MODEL):
        parts.append(w[:, o:o + width])
        o += width
    wb, wc, wu, wq, wk, wv, wqi, wki, wwi, wg = parts
    z = jnp.zeros_like(wki)
    w_att = jnp.concatenate([wq, wqi, wk, wv, wki, z, z, wki], axis=1).astype(BF16)
    w_wi = jnp.pad(wwi, ((0, 0), (0, LANES - IDX_HEADS))).astype(BF16)
    return (wb.astype(BF16), wc.astype(BF16), wu.astype(BF16), w_att, w_wi,
            wg[:, :D_MODEL].astype(BF16), wg[:, D_MODEL:].astype(BF16))


@jax.jit
def _forward(x, mix_norm, w_in, conv_w, w_conv_out, w_attn_out, w_out, ffn_norm,
             ffn_w1, ffn_w3, ffn_w2, router_w, exp_w1, exp_w3, exp_w2, final_norm):
    bsz, seq, d = x.shape
    n = bsz * seq
    depth = w_in.shape[0]
    xf = x.reshape(n, d)
    for layer in range(depth):
        wb, wc, wu, w_att, w_wi, wgc, wga = _in_proj_weights(w_in[layer])
        h = rmsnorm(xf, mix_norm[layer], BF16)
        yc = conv_branch(h, wb, wc, wu, conv_w[layer], seq)
        att_in = matmul(h, w_att, BF16, 1024, w_att.shape[1] // 2, name="attn_in_proj")
        w_idx = matmul(h, w_wi, F32, 1024, LANES, name="idx_weight_proj")
        at = dsa_attention(att_in, w_idx, bsz, seq)
        mix = gated_mix(yc, at, h, w_conv_out[layer].astype(BF16), w_attn_out[layer].astype(BF16),
                        wgc, wga)
        xf = matmul(mix, w_out[layer].astype(BF16), F32, 1024, 1024, residual=xf, name="out_proj")
        j = layer // 2
        if layer % 2 == 0:
            h2 = rmsnorm(xf, ffn_norm[layer], BF16)
            a = swiglu_up(h2, ffn_w1[j].astype(BF16), ffn_w3[j].astype(BF16))
            xf = matmul(a, ffn_w2[j].astype(BF16), F32, 1024, 512, residual=xf, name="ffn_down")
        else:
            wr = jnp.pad(router_w[j], ((0, 0), (0, LANES - N_EXPERTS))).astype(BF16)
            h2, rinfo, counts = rmsnorm_router(xf, ffn_norm[layer], wr)
            last = layer == depth - 1
            xf = moe_layer(xf, h2, rinfo, counts, exp_w1[j].astype(BF16), exp_w3[j].astype(BF16),
                           exp_w2[j].astype(BF16), final_norm if last else None)
            if last:
                return xf.reshape(bsz, seq, d)
    return rmsnorm(xf, final_norm, F32).reshape(bsz, seq, d)


def kernel(x, mix_norm, w_in, conv_w, w_conv_out, w_attn_out, w_out, ffn_norm, ffn_w1, ffn_w3,
           ffn_w2, router_w, exp_w1, exp_w3, exp_w2, final_norm):
    return _forward(x, mix_norm, w_in, conv_w, w_conv_out, w_attn_out, w_out, ffn_norm,
                    ffn_w1, ffn_w3, ffn_w2, router_w, exp_w1, exp_w3, exp_w2, final_norm)
```
